```python
import math
import jax, jax.numpy as jnp
from jax import lax
import numpy as np

D_MODEL = 2048
BATCH = 2
SEQ = 4096
DEPTH = 1
DEC_BATCH = 32
DEC_SEQ = 8
PAST_LEN = 16384
PAGE_SIZE = 128

MIX_WIDTH = D_MODEL
ATT_WIDTH = MIX_WIDTH // 2
HG_WIDTH = MIX_WIDTH - ATT_WIDTH
ATT_HEADS = 8
ATT_V_DIM = ATT_WIDTH // ATT_HEADS
ATT_QK_DIM = ATT_V_DIM // 2
HG_EXPAND = 128
HG_HEADS = HG_WIDTH // HG_EXPAND
HG_K_DIM = HG_EXPAND
HG_V_DIM = HG_WIDTH // HG_HEADS
Q_BLOCK = 128
HG_CHUNK = 64
NORM_EPS = 1e-6
SUBLN_EPS = 1e-5
IN_COLS = 4 * ATT_WIDTH + 2 * HG_HEADS * HG_K_DIM + 2 * HG_WIDTH

kernel_name = "hymba_diffattn_hgrn2_adaln_step"

F32 = jnp.float32


def rms_norm(x, g, eps):
    xf = x.astype(F32)
    y = xf * lax.rsqrt(jnp.mean(xf * xf, axis=-1, keepdims=True) + eps)
    return (y * g.astype(F32)).astype(x.dtype)


def alibi_slopes():
    return jnp.asarray([2.0 ** (-8.0 * (h + 1) / ATT_HEADS) for h in range(ATT_HEADS)], F32)


def branch_inputs(x, c, norm_g, w_ada, b_ada, w_in, lb):
    bn, L, _ = x.shape
    mod = (jax.nn.silu(c) @ w_ada + b_ada)[:, None, :]
    shift, scale, res_gate = jnp.split(mod, 3, axis=-1)
    h = rms_norm(x, norm_g, NORM_EPS) * (1.0 + scale) + shift
    z = h @ w_in
    sizes = (ATT_WIDTH, ATT_WIDTH, ATT_WIDTH, ATT_WIDTH,
             HG_HEADS * HG_K_DIM, HG_HEADS * HG_K_DIM, HG_WIDTH, HG_WIDTH)
    cuts = [int(v) for v in np.cumsum(sizes)[:-1]]
    aq, ak, av, ag, hq, hf, hi, hgate = jnp.split(z, cuts, axis=-1)
    aq = aq.reshape(bn, L, ATT_HEADS, 2, ATT_QK_DIM)
    ak = ak.reshape(bn, L, ATT_HEADS, 2, ATT_QK_DIM)
    av = av.reshape(bn, L, ATT_HEADS, ATT_V_DIM)
    hq = hq.reshape(bn, L, HG_HEADS, HG_K_DIM).astype(F32)
    forget = lb + (1.0 - lb) * jax.nn.sigmoid(hf.reshape(bn, L, HG_HEADS, HG_K_DIM).astype(F32))
    hk = 1.0 - forget
    hlog = jnp.log(forget)
    hv = hi.reshape(bn, L, HG_HEADS, HG_V_DIM).astype(F32)
    return res_gate, aq, ak, av, ag, hq, hk, hlog, hv, hgate


def diff_combine(s, q_pos, key_pos, slopes, lam, v):
    dist = (q_pos[:, None] - key_pos[None, :]).astype(F32)
    bias = -slopes[:, None, None, None] * dist
    s = jnp.where(dist >= 0, s + bias, -jnp.inf)
    p = jax.nn.softmax(s, axis=-1)
    a = p[..., 0, :, :] - lam * p[..., 1, :, :]
    return jnp.einsum('...hqk,...khv->...qhv', a, v.astype(F32))


def diff_attention_prompt(q, k, v, slopes, lam):
    bn, S = q.shape[:2]
    n_blk = S // Q_BLOCK
    scale = ATT_QK_DIM ** -0.5
    q_blocks = q.reshape(bn, n_blk, Q_BLOCK, ATT_HEADS, 2, ATT_QK_DIM).swapaxes(0, 1)
    key_pos = jnp.arange(S)

    def one_block(args):
        blk, q_blk = args
        q_pos = blk * Q_BLOCK + jnp.arange(Q_BLOCK)
        s = jnp.einsum('bqhjd,bkhjd->bhjqk', q_blk, k, preferred_element_type=F32) * scale
        return diff_combine(s, q_pos, key_pos, slopes, lam, v)

    out = lax.map(one_block, (jnp.arange(n_blk), q_blocks))
    return out.swapaxes(0, 1).reshape(bn, S, ATT_HEADS, ATT_V_DIM)


def diff_attention_sample(q, k, v, cache_k, cache_v, layer, page_table, slopes, lam):
    T = q.shape[1]
    past = page_table.shape[1] * PAGE_SIZE
    scale = ATT_QK_DIM ** -0.5
    q_pos = past + jnp.arange(T)
    key_pos = jnp.arange(past + T)

    def one_seq(args):
        q_s, k_s, v_s, pages = args
        k_all = jnp.concatenate(
            [cache_k[layer, pages].reshape(past, ATT_HEADS, 2, ATT_QK_DIM), k_s.astype(cache_k.dtype)], axis=0)
        v_all = jnp.concatenate(
            [cache_v[layer, pages].reshape(past, ATT_HEADS, ATT_V_DIM), v_s.astype(cache_v.dtype)], axis=0)
        s = jnp.einsum('qhjd,khjd->hjqk', q_s, k_all, preferred_element_type=F32) * scale
        return diff_combine(s, q_pos, key_pos, slopes, lam, v_all)

    return lax.map(one_seq, (q, k, v, page_table))


def hgrn2_scan(q, k, g, v, s0):
    bn, L = q.shape[:2]
    C = math.gcd(L, HG_CHUNK)
    n = L // C

    def to_chunks(t):
        return t.reshape(bn, n, C, t.shape[2], t.shape[3]).transpose(1, 0, 3, 2, 4)

    causal = jnp.tril(jnp.ones((C, C), bool))

    def step(s, xs):
        qc, kc, gc, vc = xs
        b = jnp.cumsum(gc, axis=2)
        o_inter = jnp.einsum('bhck,bhkv->bhcv', qc * jnp.exp(b), s)
        rel = jnp.where(causal[:, :, None], b[:, :, :, None, :] - b[:, :, None, :, :], -jnp.inf)
        a = jnp.einsum('bhtk,bhtsk,bhsk->bhts', qc, jnp.exp(rel), kc)
        o = o_inter + jnp.einsum('bhts,bhsv->bhtv', a, vc)
        b_last = b[:, :, -1:, :]
        s_new = jnp.exp(b_last[:, :, 0, :])[..., None] * s + jnp.einsum(
            'bhck,bhcv->bhkv', kc * jnp.exp(b_last - b), vc)
        return s_new, o

    s_fin, o = lax.scan(step, s0, (to_chunks(q), to_chunks(k), to_chunks(g), to_chunks(v)))
    o = o.transpose(1, 0, 3, 2, 4).reshape(bn, L, HG_HEADS, HG_V_DIM)
    return o, s_fin


def mixer_outputs(x, res_gate, att_o, att_gate, hg_o, hg_gate, subln_g, hg_norm_g, w_out, lam_init):
    bn, L, _ = x.shape
    att = (rms_norm(att_o, subln_g, SUBLN_EPS) * (1.0 - lam_init)).reshape(bn, L, ATT_WIDTH)
    hg = rms_norm(hg_o, hg_norm_g, NORM_EPS).reshape(bn, L, HG_WIDTH)
    mixed = jnp.concatenate([att.astype(x.dtype) * jax.nn.silu(att_gate),
                             hg.astype(x.dtype) * jax.nn.silu(hg_gate)], axis=-1)
    return x + res_gate * (mixed @ w_out)


def setup_inputs(seed: int = 0) -> dict:
    key = jax.random.key(seed)
    ks = jax.random.split(key, 24)
    n_pages = PAST_LEN // PAGE_SIZE
    n_pool = (DEC_BATCH * n_pages * 5) // 4

    def nrm(k, shape, scale):
        return jax.random.normal(k, shape, F32) * scale

    page_table = jax.random.permutation(ks[5], n_pool)[: DEC_BATCH * n_pages]
    page_table = page_table.reshape(DEC_BATCH, n_pages).astype(jnp.int32)
    return {
        "x_prompt": nrm(ks[0], (BATCH, SEQ, D_MODEL), 1.0),
        "x_sample": nrm(ks[1], (DEC_BATCH, DEC_SEQ, D_MODEL), 1.0),
        "cache_k": nrm(ks[2], (DEPTH, n_pool, PAGE_SIZE, ATT_HEADS, 2, ATT_QK_DIM), 1.0),
        "cache_v": nrm(ks[3], (DEPTH, n_pool, PAGE_SIZE, ATT_HEADS, ATT_V_DIM), 1.0),
        "state_hgrn": nrm(ks[4], (DEPTH, DEC_BATCH, HG_HEADS, HG_K_DIM, HG_V_DIM), 0.5),
        "page_table": page_table,
        "c_prompt": nrm(ks[6], (BATCH, D_MODEL), 1.0),
        "c_sample": nrm(ks[7], (DEC_BATCH, D_MODEL), 1.0),
        "norm_g": 1.0 + nrm(ks[8], (DEPTH, D_MODEL), 0.02),
        "w_ada": nrm(ks[9], (DEPTH, D_MODEL, 3 * D_MODEL), 0.2 * D_MODEL ** -0.5),
        "b_ada": nrm(ks[10], (DEPTH, 3 * D_MODEL), 0.02),
        "w_in": nrm(ks[11], (DEPTH, D_MODEL, IN_COLS), D_MODEL ** -0.5),
        "lambda_q1": nrm(ks[12], (DEPTH, ATT_QK_DIM), 0.1),
        "lambda_k1": nrm(ks[13], (DEPTH, ATT_QK_DIM), 0.1),
        "lambda_q2": nrm(ks[14], (DEPTH, ATT_QK_DIM), 0.1),
        "lambda_k2": nrm(ks[15], (DEPTH, ATT_QK_DIM), 0.1),
        "subln_g": 1.0 + nrm(ks[16], (DEPTH, ATT_V_DIM), 0.02),
        "hg_lower_bounds": nrm(ks[17], (DEPTH + 1, HG_HEADS * HG_K_DIM), 0.1),
        "hg_norm_g": 1.0 + nrm(ks[18], (DEPTH, HG_V_DIM), 0.02),
        "w_out": nrm(ks[19], (DEPTH, MIX_WIDTH, D_MODEL), MIX_WIDTH ** -0.5),
        "final_g": 1.0 + nrm(ks[20], (D_MODEL,), 0.02),
    }


def reference(x_prompt, x_sample, cache_k, cache_v, state_hgrn, page_table, c_prompt, c_sample,
              norm_g, w_ada, b_ada, w_in, lambda_q1, lambda_k1, lambda_q2, lambda_k2, subln_g,
              hg_lower_bounds, hg_norm_g, w_out, final_g):
    slopes = alibi_slopes()
    lb_all = jnp.cumsum(jax.nn.softmax(hg_lower_bounds.astype(F32), axis=0), axis=0)
    xp, xs = x_prompt, x_sample
    kp_l, vp_l, sp_l, ks_l, vs_l, ss_l = [], [], [], [], [], []
    for l in range(DEPTH):
        lam_init = 0.8 - 0.6 * math.exp(-0.3 * l)
        lam = (jnp.exp(jnp.sum(lambda_q1[l].astype(F32) * lambda_k1[l].astype(F32)))
               - jnp.exp(jnp.sum(lambda_q2[l].astype(F32) * lambda_k2[l].astype(F32))) + lam_init)
        lb = lb_all[l].reshape(HG_HEADS, HG_K_DIM)

        rg, aq, ak, av, ag, hq, hk, hlog, hv, hgt = branch_inputs(
            xp, c_prompt, norm_g[l], w_ada[l], b_ada[l], w_in[l], lb)
        att_o = diff_attention_prompt(aq, ak, av, slopes, lam)
        s0 = jnp.zeros((xp.shape[0], HG_HEADS, HG_K_DIM, HG_V_DIM), F32)
        hg_o, s_p = hgrn2_scan(hq, hk, hlog, hv, s0)
        xp = mixer_outputs(xp, rg, att_o, ag, hg_o, hgt, subln_g[l], hg_norm_g[l], w_out[l], lam_init)
        kp_l.append(ak.astype(cache_k.dtype))
        vp_l.append(av.astype(cache_v.dtype))
        sp_l.append(s_p.astype(state_hgrn.dtype))

        rg, aq, ak, av, ag, hq, hk, hlog, hv, hgt = branch_inputs(
            xs, c_sample, norm_g[l], w_ada[l], b_ada[l], w_in[l], lb)
        att_o = diff_attention_sample(aq, ak, av, cache_k, cache_v, l, page_table, slopes, lam)
        hg_o, s_s = hgrn2_scan(hq, hk, hlog, hv, state_hgrn[l].astype(F32))
        xs = mixer_outputs(xs, rg, att_o, ag, hg_o, hgt, subln_g[l], hg_norm_g[l], w_out[l], lam_init)
        ks_l.append(ak.astype(cache_k.dtype))
        vs_l.append(av.astype(cache_v.dtype))
        ss_l.append(s_s.astype(state_hgrn.dtype))

    y_prompt = rms_norm(xp, final_g, NORM_EPS)
    y_sample = rms_norm(xs, final_g, NORM_EPS)
    return (y_prompt, y_sample, jnp.stack(kp_l), jnp.stack(vp_l), jnp.stack(sp_l),
            jnp.stack(ks_l), jnp.stack(vs_l), jnp.stack(ss_l))
```

```python
import functools
import math

import jax
import jax.numpy as jnp
from jax import lax
from jax.experimental import pallas as pl
from jax.experimental.pallas import tpu as pltpu

F32 = jnp.float32
BF16 = jnp.bfloat16

NORM_EPS = 1e-6
SUBLN_EPS = 1e-5
N_SECTIONS = 8
HEAD_DIM = 128
QK_DIM = 64
HG_CHUNK = 16
VMEM_LIMIT_BYTES = 56 * 1024 * 1024

_NT = (((1,), (1,)), ((), ()))
_TN = (((0,), (0,)), ((), ()))


def _params(*sem):
    return pltpu.CompilerParams(dimension_semantics=sem, vmem_limit_bytes=VMEM_LIMIT_BYTES)


def _silu(x):
    return x * jax.nn.sigmoid(x)


def _adaln_kernel(c_ref, w_ref, b_ref, o_ref):
    a = _silu(c_ref[...]).astype(BF16)
    o_ref[...] = jnp.dot(a, w_ref[...].astype(BF16), preferred_element_type=F32) + b_ref[...]


def _adaln(c, w, b, tn=1024):
    m, d = c.shape
    n = w.shape[1]
    return pl.pallas_call(
        _adaln_kernel,
        out_shape=jax.ShapeDtypeStruct((m, n), F32),
        grid=(n // tn,),
        in_specs=[pl.BlockSpec((m, d), lambda j: (0, 0)),
                  pl.BlockSpec((d, tn), lambda j: (0, j)),
                  pl.BlockSpec((1, tn), lambda j: (0, j))],
        out_specs=pl.BlockSpec((m, tn), lambda j: (0, j)),
        compiler_params=_params("arbitrary"),
        name="adaln",
    )(c, w, b.reshape(1, n))


def _modnorm_kernel(x_ref, g_ref, sc_ref, sh_ref, h_ref):
    x = x_ref[...]
    y = x * lax.rsqrt(jnp.mean(x * x, axis=-1, keepdims=True) + NORM_EPS) * g_ref[...]
    h_ref[...] = (y * (1.0 + sc_ref[0]) + sh_ref[0]).astype(h_ref.dtype)


def _modnorm(x, g, scale, shift, tm, rows_per_mod):
    r, d = x.shape
    rb = scale.shape[1]
    tiles_per_mod = rows_per_mod // tm
    mod_spec = pl.BlockSpec((1, rb, d), lambda i: (i // tiles_per_mod, 0, 0))
    return pl.pallas_call(
        _modnorm_kernel,
        out_shape=jax.ShapeDtypeStruct((r, d), BF16),
        grid=(r // tm,),
        in_specs=[pl.BlockSpec((tm, d), lambda i: (i, 0)),
                  pl.BlockSpec((1, d), lambda i: (0, 0)),
                  mod_spec, mod_spec],
        out_specs=pl.BlockSpec((tm, d), lambda i: (i, 0)),
        compiler_params=_params("arbitrary"),
        name="modnorm",
    )(x, g.reshape(1, d), scale, shift)


def _inproj_kernel(h_ref, w_ref, o_ref):
    o_ref[0] = jnp.dot(h_ref[...], w_ref[...], preferred_element_type=F32)


def _inproj(h, w, tm):
    r, d = h.shape
    wsec = w.shape[1] // N_SECTIONS
    return pl.pallas_call(
        _inproj_kernel,
        out_shape=jax.ShapeDtypeStruct((N_SECTIONS, r, wsec), F32),
        grid=(r // tm, N_SECTIONS),
        in_specs=[pl.BlockSpec((tm, d), lambda i, s: (i, 0)),
                  pl.BlockSpec((d, wsec), lambda i, s: (0, s))],
        out_specs=pl.BlockSpec((1, tm, wsec), lambda i, s: (s, i, 0)),
        compiler_params=_params("arbitrary", "arbitrary"),
        name="inproj",
    )(h, w)


def _inproj_prompt_kernel(h_ref, w_ref, qkv_ref, gh_ref, kt_ref, vo_ref, *, n_heads):
    s = pl.program_id(1)
    tm = h_ref.shape[0]
    z = jnp.dot(h_ref[...], w_ref[...], preferred_element_type=F32)

    @pl.when(s <= 2)
    def _():
        qkv_ref[0] = z.astype(BF16)

    @pl.when(s == 1)
    def _():
        kt_ref[0] = z.T

    @pl.when(s == 2)
    def _():
        for hh in range(n_heads):
            vo_ref[pl.ds(hh, tm, stride=n_heads), :] = z[:, hh * HEAD_DIM:(hh + 1) * HEAD_DIM]

    @pl.when(s >= 3)
    def _():
        gh_ref[0] = z


def _inproj_prompt(h, w, batch, seq, n_heads, tm):
    r, d = h.shape
    wsec = w.shape[1] // N_SECTIONS
    tpb = seq // tm
    kern = functools.partial(_inproj_prompt_kernel, n_heads=n_heads)
    return pl.pallas_call(
        kern,
        out_shape=(jax.ShapeDtypeStruct((3, r, wsec), BF16),
                   jax.ShapeDtypeStruct((N_SECTIONS - 3, r, wsec), F32),
                   jax.ShapeDtypeStruct((batch, wsec, seq), F32),
                   jax.ShapeDtypeStruct((r * n_heads, HEAD_DIM), F32)),
        grid=(r // tm, N_SECTIONS),
        in_specs=[pl.BlockSpec((tm, d), lambda i, s: (i, 0)),
                  pl.BlockSpec((d, wsec), lambda i, s: (0, s))],
        out_specs=(pl.BlockSpec((1, tm, wsec), lambda i, s: (jnp.minimum(s, 2), i, 0)),
                   pl.BlockSpec((1, tm, wsec), lambda i, s: (jnp.maximum(s, 3) - 3, i, 0)),
                   pl.BlockSpec((1, wsec, tm), lambda i, s: (i // tpb, 0, i % tpb)),
                   pl.BlockSpec((tm * n_heads, HEAD_DIM), lambda i, s: (i, 0))),
        compiler_params=_params("arbitrary", "arbitrary"),
        name="inproj_prompt",
    )(h, w)


def _attn_prompt_kernel(par_ref, q_ref, k_ref, v_ref, g_ref, sg_ref, o_ref,
                        vt, nb_s, nbd_s, *, tq, nk, out_scale):
    h = pl.program_id(1)
    qt = pl.program_id(2)
    lam = par_ref[0]
    slope = par_ref[1 + h]

    @pl.when(qt == 0)
    def _fill():
        for j in range(nk):
            vt[j] = v_ref[0, j * tq:(j + 1) * tq, :].astype(F32).T.astype(BF16)
        jj = lax.broadcasted_iota(jnp.int32, (tq, 2 * tq), 0)
        ii = lax.broadcasted_iota(jnp.int32, (tq, 2 * tq), 1)
        ii = jnp.where(ii >= tq, ii - tq, ii)
        nb = -slope * (ii - jj).astype(F32)
        nb_s[...] = nb
        nbd_s[...] = jnp.where(ii >= jj, nb, -jnp.inf)

    q = q_ref[0].astype(F32) * (QK_DIM ** -0.5)
    lane = lax.broadcasted_iota(jnp.int32, q.shape, 1)
    qs = jnp.concatenate([jnp.where(lane < QK_DIM, q, 0.0), jnp.where(lane >= QK_DIM, q, 0.0)],
                         axis=0).astype(BF16)

    def keys(kt):
        return k_ref[0, pl.ds(pl.multiple_of(kt * tq, tq), tq), :]

    s = lax.dot_general(keys(qt), qs, _NT, preferred_element_type=F32) + nbd_s[...]
    m = jnp.max(s, axis=0, keepdims=True)
    p = jnp.exp(s - m)
    l = jnp.sum(p, axis=0, keepdims=True)
    acc = jnp.dot(vt[qt], p.astype(BF16), preferred_element_type=F32)

    def body(kt, carry):
        m, l, acc = carry
        s = lax.dot_general(keys(kt), qs, _NT, preferred_element_type=F32) + nb_s[...]
        c = slope * (jnp.zeros((1, 2 * tq), jnp.int32) + (kt - qt) * tq).astype(F32)
        m_new = jnp.maximum(m, jnp.max(s, axis=0, keepdims=True) + c)
        alpha = jnp.exp(m - m_new)
        p = jnp.exp(s - (m_new - c))
        l = alpha * l + jnp.sum(p, axis=0, keepdims=True)
        acc = alpha * acc + jnp.dot(vt[kt], p.astype(BF16), preferred_element_type=F32)
        return m_new, l, acc

    m, l, acc = lax.fori_loop(0, qt, body, (m, l, acc))

    o = acc / l
    od = (o[:, :tq] - lam * o[:, tq:]).T
    y = od * lax.rsqrt(jnp.mean(od * od, axis=-1, keepdims=True) + SUBLN_EPS) * sg_ref[...] * out_scale
    o_ref[...] = (y * _silu(g_ref[0])).astype(o_ref.dtype)


def _attn_prompt(qkv, gh, par, subln_g, batch, seq, n_heads, out_scale, tq=256):
    nq = seq // tq
    kern = functools.partial(_attn_prompt_kernel, tq=tq, nk=nq, out_scale=out_scale)
    return pl.pallas_call(
        kern,
        out_shape=jax.ShapeDtypeStruct((batch * seq, n_heads * HEAD_DIM), BF16),
        grid=(batch, n_heads, nq),
        in_specs=[pl.BlockSpec(memory_space=pltpu.SMEM),
                  pl.BlockSpec((1, tq, HEAD_DIM), lambda b, h, i: (0, b * nq + i, h)),
                  pl.BlockSpec((1, seq, HEAD_DIM), lambda b, h, i: (1, b, h)),
                  pl.BlockSpec((1, seq, HEAD_DIM), lambda b, h, i: (2, b, h)),
                  pl.BlockSpec((1, tq, HEAD_DIM), lambda b, h, i: (0, b * nq + i, h)),
                  pl.BlockSpec((1, HEAD_DIM), lambda b, h, i: (0, 0))],
        out_specs=pl.BlockSpec((tq, HEAD_DIM), lambda b, h, i: (b * nq + i, h)),
        scratch_shapes=[pltpu.VMEM((nq, HEAD_DIM, tq), BF16),
                        pltpu.VMEM((tq, 2 * tq), F32),
                        pltpu.VMEM((tq, 2 * tq), F32)],
        compiler_params=_params("arbitrary", "arbitrary", "arbitrary"),
        name="attn_prompt",
    )(par, qkv, qkv, qkv, gh, subln_g.reshape(1, HEAD_DIM))


def _attn_sample_kernel(pt_ref, par_ref, slope_ref, q_ref, kn_ref, vn_ref, g_ref, sg_ref, *rest,
                        npg, page, n_new, n_heads, out_scale):
    k_refs = rest[:npg]
    v_refs = rest[npg:2 * npg]
    o_ref = rest[2 * npg]
    qbd, kbf, vbf, b0_s, m_s, l_s, acc_s = rest[2 * npg + 1:]
    i = pl.program_id(1)
    n_steps = pl.num_programs(1)
    tblk = npg * page
    nrow, width = qbd.shape
    rows_per_head = 2 * n_new
    slope = slope_ref[...]

    def head_rows(hh):
        return slice(hh * rows_per_head, (hh + 1) * rows_per_head)

    @pl.when(i == 0)
    def _init():
        q = q_ref[0] * (QK_DIM ** -0.5)
        qt = jnp.concatenate([q] * (nrow // n_new), axis=0)
        rr = lax.broadcasted_iota(jnp.int32, (nrow, width), 0)
        cc = lax.broadcasted_iota(jnp.int32, (nrow, width), 1)
        qbd[...] = jnp.where(rr // n_new == cc // QK_DIM, qt, 0.0).astype(BF16)
        b0_s[...] = slope * lax.broadcasted_iota(jnp.int32, (nrow, tblk), 1).astype(F32)
        zpad = jnp.zeros((nrow - n_new, width), F32)
        kn = jnp.concatenate([kn_ref[0], zpad], axis=0).astype(BF16)
        vn = jnp.concatenate([vn_ref[0], zpad], axis=0).astype(BF16)
        s = lax.dot_general(qbd[...], kn, _NT, preferred_element_type=F32)
        tk = lax.broadcasted_iota(jnp.int32, (nrow, nrow), 1)
        tqry = lax.broadcasted_iota(jnp.int32, (nrow, nrow), 0) % n_new
        s = jnp.where(tk <= tqry, s + slope * tk.astype(F32), -jnp.inf)
        m = jnp.max(s, axis=1, keepdims=True)
        p = jnp.exp(s - m)
        m_s[...] = m
        l_s[...] = jnp.sum(p, axis=1, keepdims=True)
        pb = p.astype(BF16)
        for hh in range(n_heads):
            acc_s[head_rows(hh), :] = jnp.dot(pb[head_rows(hh), :], vn[:, hh * HEAD_DIM:(hh + 1) * HEAD_DIM],
                                              preferred_element_type=F32)

    for pg in range(npg):
        kbf[:, pg * page:(pg + 1) * page] = k_refs[pg][...].astype(BF16)
        for hh in range(n_heads):
            vbf[hh, pg * page:(pg + 1) * page, :] = v_refs[pg][pl.ds(hh, page, stride=n_heads), :].astype(BF16)
    s = jnp.dot(qbd[...], kbf[...], preferred_element_type=F32) + b0_s[...]
    c = slope * (jnp.zeros((nrow, 1), jnp.int32) + (i - n_steps) * tblk).astype(F32)
    m_old = m_s[...]
    m_new = jnp.maximum(m_old, jnp.max(s, axis=1, keepdims=True) + c)
    alpha = jnp.exp(m_old - m_new)
    p = jnp.exp(s - (m_new - c))
    m_s[...] = m_new
    l_s[...] = alpha * l_s[...] + jnp.sum(p, axis=1, keepdims=True)
    pb = p.astype(BF16)
    for hh in range(n_heads):
        rs = head_rows(hh)
        acc_s[rs, :] = alpha[rs, :] * acc_s[rs, :] + jnp.dot(pb[rs, :], vbf[hh], preferred_element_type=F32)

    @pl.when(i == n_steps - 1)
    def _finish():
        lam = par_ref[0]
        o = acc_s[...] / l_s[...]
        g = g_ref[0]
        for hh in range(n_heads):
            cs = slice(hh * HEAD_DIM, (hh + 1) * HEAD_DIM)
            blk = o[head_rows(hh), :]
            od = blk[:n_new] - lam * blk[n_new:]
            y = od * lax.rsqrt(jnp.mean(od * od, axis=-1, keepdims=True) + SUBLN_EPS) * sg_ref[...] * out_scale
            o_ref[0, :, cs] = y * _silu(g[:, cs])


def _attn_sample(z, cache_kt, cache_v, page_table, par, subln_g, n_heads, n_new, out_scale, npg=8):
    dec_batch, n_pages = page_table.shape
    width, page = cache_kt.shape[1], cache_kt.shape[2]
    nrow = n_heads * 2 * n_new
    assert nrow == HEAD_DIM and n_pages % npg == 0
    tblk = npg * page
    row_head = jnp.arange(nrow) // (2 * n_new)
    slope_col = jnp.exp2(-8.0 * (row_head + 1).astype(F32) / n_heads).reshape(nrow, 1)

    def page_spec(pg, rows, cols):
        return pl.BlockSpec((None, rows, cols), lambda s, i, pt: (pt[s, i * npg + pg], 0, 0))

    def sec_spec(sec):
        return pl.BlockSpec((1, None, n_new, width), lambda s, i, pt: (sec, s, 0, 0))

    kern = functools.partial(_attn_sample_kernel, npg=npg, page=page, n_new=n_new, n_heads=n_heads,
                             out_scale=out_scale)
    grid_spec = pltpu.PrefetchScalarGridSpec(
        num_scalar_prefetch=1,
        grid=(dec_batch, n_pages // npg),
        in_specs=[pl.BlockSpec(memory_space=pltpu.SMEM),
                  pl.BlockSpec((nrow, 1), lambda s, i, pt: (0, 0)),
                  sec_spec(0), sec_spec(1), sec_spec(2), sec_spec(3),
                  pl.BlockSpec((1, HEAD_DIM), lambda s, i, pt: (0, 0))]
                 + [page_spec(pg, width, page) for pg in range(npg)]
                 + [page_spec(pg, page * n_heads, HEAD_DIM) for pg in range(npg)],
        out_specs=pl.BlockSpec((1, n_new, width), lambda s, i, pt: (s, 0, 0)),
        scratch_shapes=[pltpu.VMEM((nrow, width), BF16),
                        pltpu.VMEM((width, tblk), BF16),
                        pltpu.VMEM((n_heads, tblk, HEAD_DIM), BF16),
                        pltpu.VMEM((nrow, tblk), F32),
                        pltpu.VMEM((nrow, 1), F32),
                        pltpu.VMEM((nrow, 1), F32),
                        pltpu.VMEM((nrow, HEAD_DIM), F32)],
    )
    return pl.pallas_call(
        kern,
        out_shape=jax.ShapeDtypeStruct((dec_batch, n_new, width), F32),
        grid_spec=grid_spec,
        compiler_params=_params("arbitrary", "arbitrary"),
        name="attn_sample",
    )(page_table, par, slope_col, z, z, z, z, subln_g.reshape(1, HEAD_DIM),
      *([cache_kt] * npg), *([cache_v] * npg))


def _hgrn_kernel(*refs, chunk, n_heads, has_init):
    if has_init:
        q_ref, f_ref, v_ref, g_ref, lb_ref, ng_ref, s0_ref, o_ref, sout_ref, st, b_s, k_s = refs
    else:
        q_ref, f_ref, v_ref, g_ref, lb_ref, ng_ref, o_ref, sout_ref, st, b_s, k_s = refs
        s0_ref = None
    i = pl.program_id(1)
    tb = f_ref.shape[1]

    @pl.when(i == 0)
    def _init():
        for hh in range(n_heads):
            st[hh] = s0_ref[0, hh].T if has_init else jnp.zeros((HEAD_DIM, HEAD_DIM), F32)

    lb = lb_ref[...]
    forget = lb + (1.0 - lb) * jax.nn.sigmoid(f_ref[0])
    k_s[...] = 1.0 - forget
    b = jnp.log(forget)
    row = lax.broadcasted_iota(jnp.int32, b.shape, 0) % chunk
    sh = 1
    while sh < chunk:
        b = b + jnp.where(row >= sh, pltpu.roll(b, sh, axis=0), 0.0)
        sh *= 2
    b_s[...] = b

    tt = lax.broadcasted_iota(jnp.int32, (chunk, chunk, HEAD_DIM), 0)
    ss = lax.broadcasted_iota(jnp.int32, (chunk, chunk, HEAD_DIM), 1)
    causal = tt >= ss

    def body(ci, carry):
        r0 = pl.multiple_of(ci * chunk, chunk)
        rs = pl.ds(r0, chunk)
        for hh in range(n_heads):
            cs = slice(hh * HEAD_DIM, (hh + 1) * HEAD_DIM)
            bq = b_s[rs, cs]
            qh = q_ref[0, rs, cs]
            kh = k_s[rs, cs]
            vh = v_ref[0, rs, cs]
            blast = bq[chunk - 1:chunk, :]
            sth = st[hh]
            o = lax.dot_general((qh * jnp.exp(bq)).astype(BF16), sth.astype(BF16), _NT,
                                preferred_element_type=F32)
            rel = jnp.where(causal, bq[:, None, :] - bq[None, :, :], -jnp.inf)
            a = jnp.sum(qh[:, None, :] * jnp.exp(rel) * kh[None, :, :], axis=-1)
            o = o + jnp.dot(a.astype(BF16), vh.astype(BF16), preferred_element_type=F32)
            y = o * lax.rsqrt(jnp.mean(o * o, axis=-1, keepdims=True) + NORM_EPS) * ng_ref[...]
            o_ref[0, rs, cs] = (y * _silu(g_ref[0, rs, cs])).astype(o_ref.dtype)
            ke = (kh * jnp.exp(blast - bq)).astype(BF16)
            st[hh] = sth * jnp.exp(blast) + lax.dot_general(vh.astype(BF16), ke, _TN,
                                                            preferred_element_type=F32)
        return carry

    lax.fori_loop(0, tb // chunk, body, 0)

    @pl.when(i == pl.num_programs(1) - 1)
    def _emit():
        for hh in range(n_heads):
            sout_ref[0, hh] = st[hh].T


def _hgrn(z, sec0, lb, norm_g, s0, batch, length, n_heads, tb, chunk, out_dtype):
    width = z.shape[2]
    nblk = length // tb
    has_init = s0 is not None

    def sec_spec(sec):
        return pl.BlockSpec((1, tb, width), lambda b, i: (sec, b * nblk + i, 0))

    in_specs = [sec_spec(sec0), sec_spec(sec0 + 1), sec_spec(sec0 + 2), sec_spec(sec0 + 3),
                pl.BlockSpec((1, width), lambda b, i: (0, 0)),
                pl.BlockSpec((1, HEAD_DIM), lambda b, i: (0, 0))]
    args = [z, z, z, z, lb.reshape(1, width), norm_g.reshape(1, HEAD_DIM)]
    state_spec = pl.BlockSpec((1, n_heads, HEAD_DIM, HEAD_DIM), lambda b, i: (b, 0, 0, 0))
    if has_init:
        in_specs.append(state_spec)
        args.append(s0)
    kern = functools.partial(_hgrn_kernel, chunk=chunk, n_heads=n_heads, has_init=has_init)
    return pl.pallas_call(
        kern,
        out_shape=(jax.ShapeDtypeStruct((1, batch * length, width), out_dtype),
                   jax.ShapeDtypeStruct((batch, n_heads, HEAD_DIM, HEAD_DIM), F32)),
        grid=(batch, nblk),
        in_specs=in_specs,
        out_specs=(pl.BlockSpec((1, tb, width), lambda b, i: (0, b * nblk + i, 0)), state_spec),
        scratch_shapes=[pltpu.VMEM((n_heads, HEAD_DIM, HEAD_DIM), F32),
                        pltpu.VMEM((tb, width), F32),
                        pltpu.VMEM((tb, width), F32)],
        compiler_params=_params("arbitrary", "arbitrary"),
        name="hgrn",
    )(*args)


def _outproj_kernel(ma_ref, mh_ref, w_ref, x_ref, rg_ref, fg_ref, y_ref, *, final_norm):
    half = ma_ref.shape[1]
    u = jnp.dot(ma_ref[...].astype(BF16), w_ref[:half, :], preferred_element_type=F32)
    u = u + jnp.dot(mh_ref[...].astype(BF16), w_ref[half:, :], preferred_element_type=F32)
    r = x_ref[...] + rg_ref[0] * u
    if final_norm:
        r = r * lax.rsqrt(jnp.mean(r * r, axis=-1, keepdims=True) + NORM_EPS) * fg_ref[...]
    y_ref[...] = r


def _outproj(m_att, m_hg, w_bf, x, rg, final_g, tm, rows_per_mod, final_norm):
    r, d = x.shape
    half = m_att.shape[1]
    rb = rg.shape[1]
    tiles_per_mod = rows_per_mod // tm
    return pl.pallas_call(
        functools.partial(_outproj_kernel, final_norm=final_norm),
        out_shape=jax.ShapeDtypeStruct((r, d), F32),
        grid=(r // tm,),
        in_specs=[pl.BlockSpec((tm, half), lambda i: (i, 0)),
                  pl.BlockSpec((tm, half), lambda i: (i, 0)),
                  pl.BlockSpec((2 * half, d), lambda i: (0, 0)),
                  pl.BlockSpec((tm, d), lambda i: (i, 0)),
                  pl.BlockSpec((1, rb, d), lambda i: (i // tiles_per_mod, 0, 0)),
                  pl.BlockSpec((1, d), lambda i: (0, 0))],
        out_specs=pl.BlockSpec((tm, d), lambda i: (i, 0)),
        compiler_params=_params("arbitrary"),
        name="outproj",
    )(m_att, m_hg, w_bf, x, rg, final_g.reshape(1, d))


def kernel(x_prompt, x_sample, cache_k, cache_v, state_hgrn, page_table, c_prompt, c_sample,
           norm_g, w_ada, b_ada, w_in, lambda_q1, lambda_k1, lambda_q2, lambda_k2, subln_g,
           hg_lower_bounds, hg_norm_g, w_out, final_g):
    batch, seq, d = x_prompt.shape
    dec_batch, n_new, _ = x_sample.shape
    depth, n_pool, page, n_heads, _, qk_dim = cache_k.shape
    hg_heads = state_hgrn.shape[2]
    width = d // 2
    assert qk_dim == QK_DIM and n_heads * HEAD_DIM == width and hg_heads * HEAD_DIM == width
    assert w_in.shape[2] == N_SECTIONS * width

    slopes = jnp.asarray([2.0 ** (-8.0 * (h + 1) / n_heads) for h in range(n_heads)], F32)
    lb_all = jnp.cumsum(jax.nn.softmax(hg_lower_bounds.astype(F32), axis=0), axis=0)
    n_mod = batch + dec_batch
    n_mod_pad = -(-n_mod // 8) * 8
    c_all = jnp.concatenate([c_prompt, c_sample, jnp.zeros((n_mod_pad - n_mod, d), F32)], axis=0)

    xp = x_prompt.reshape(batch * seq, d)
    xs = x_sample.reshape(dec_batch * n_new, d)
    rs = dec_batch * n_new
    outs = [[] for _ in range(6)]
    for l in range(depth):
        last = l == depth - 1
        lam_init = 0.8 - 0.6 * math.exp(-0.3 * l)
        lam = (jnp.exp(jnp.sum(lambda_q1[l].astype(F32) * lambda_k1[l].astype(F32)))
               - jnp.exp(jnp.sum(lambda_q2[l].astype(F32) * lambda_k2[l].astype(F32))) + lam_init)
        par = jnp.concatenate([lam.reshape(1), slopes, jnp.zeros((7,), F32)])
        w_in_bf = w_in[l].astype(BF16)
        w_out_bf = w_out[l].astype(BF16)

        mod = _adaln(c_all, w_ada[l], b_ada[l])
        shift, scale, rgate = mod[:, :d], mod[:, d:2 * d], mod[:, 2 * d:]

        def rows_p(t):
            return t[:batch].reshape(batch, 1, d)

        def rows_s(t):
            return jnp.repeat(t[batch:n_mod], n_new, axis=0).reshape(1, rs, d)

        hp = _modnorm(xp, norm_g[l], rows_p(scale), rows_p(shift), tm=512, rows_per_mod=seq)
        qkv, gh, k_t, v_rows = _inproj_prompt(hp, w_in_bf, batch, seq, n_heads, tm=min(seq, 1024))
        att_p = _attn_prompt(qkv, gh, par, subln_g[l], batch, seq, n_heads, 1.0 - lam_init)
        hg_p, s_p = _hgrn(gh, 1, lb_all[l], hg_norm_g[l], None, batch, seq, hg_heads,
                          tb=256, chunk=HG_CHUNK, out_dtype=BF16)
        xp = _outproj(att_p, hg_p[0], w_out_bf, xp, rows_p(rgate), final_g,
                      tm=256, rows_per_mod=seq, final_norm=last)
        k_p = jnp.transpose(k_t.reshape(batch, n_heads, 2, QK_DIM, seq), (0, 4, 1, 2, 3))
        outs[0].append(k_p)
        outs[1].append(v_rows.reshape(batch, seq, n_heads, HEAD_DIM))
        outs[2].append(s_p)

        hs = _modnorm(xs, norm_g[l], rows_s(scale), rows_s(shift), tm=rs, rows_per_mod=rs)
        zs = _inproj(hs, w_in_bf, tm=rs)
        cache_kt = jnp.transpose(cache_k[l], (0, 2, 3, 4, 1)).reshape(n_pool, width, page)
        cache_vr = cache_v[l].reshape(n_pool, page * n_heads, HEAD_DIM)
        att_s = _attn_sample(zs.reshape(N_SECTIONS, dec_batch, n_new, width), cache_kt, cache_vr,
                             page_table, par, subln_g[l], n_heads, n_new, 1.0 - lam_init)
        hg_s, s_s = _hgrn(zs, 4, lb_all[l], hg_norm_g[l], state_hgrn[l], dec_batch, n_new, hg_heads,
                          tb=n_new, chunk=math.gcd(n_new, HG_CHUNK), out_dtype=F32)
        xs = _outproj(att_s.reshape(rs, width), hg_s[0], w_out_bf, xs, rows_s(rgate), final_g,
                      tm=rs, rows_per_mod=rs, final_norm=last)
        outs[3].append(zs[1].reshape(dec_batch, n_new, n_heads, 2, QK_DIM))
        outs[4].append(zs[2].reshape(dec_batch, n_new, n_heads, HEAD_DIM))
        outs[5].append(s_s)

    y_prompt = xp.reshape(batch, seq, d)
    y_sample = xs.reshape(dec_batch, n_new, d)
    return (y_prompt, y_sample) + tuple(jnp.stack(o) for o in outs)
```

```python
import functools
import math

import jax
import jax.numpy as jnp
from jax import lax
from jax.experimental import pallas as pl
from jax.experimental.pallas import tpu as pltpu

F32 = jnp.float32
BF16 = jnp.bfloat16

NORM_EPS = 1e-6
SUBLN_EPS = 1e-5
N_SECTIONS = 8
HEAD_DIM = 128
QK_DIM = 64
HG_CHUNK = 16
LOG2E = 1.4426950408889634
VMEM_LIMIT_BYTES = 56 * 1024 * 1024

_NT = (((1,), (1,)), ((), ()))
_TN = (((0,), (0,)), ((), ()))


def _params(*sem):
    return pltpu.CompilerParams(dimension_semantics=sem, vmem_limit_bytes=VMEM_LIMIT_BYTES)


def _silu(x):
    return x * jax.nn.sigmoid(x)


def _adaln_kernel(c_ref, w_ref, b_ref, o_ref):
    a = _silu(c_ref[...]).astype(BF16)
    o_ref[...] = jnp.dot(a, w_ref[...].astype(BF16), preferred_element_type=F32) + b_ref[...]


def _adaln(c, w, b, tn=1024):
    m, d = c.shape
    n = w.shape[1]
    return pl.pallas_call(
        _adaln_kernel,
        out_shape=jax.ShapeDtypeStruct((m, n), F32),
        grid=(n // tn,),
        in_specs=[pl.BlockSpec((m, d), lambda j: (0, 0)),
                  pl.BlockSpec((d, tn), lambda j: (0, j)),
                  pl.BlockSpec((1, tn), lambda j: (0, j))],
        out_specs=pl.BlockSpec((m, tn), lambda j: (0, j)),
        compiler_params=_params("arbitrary"),
        name="adaln",
    )(c, w, b.reshape(1, n))


def _modnorm_kernel(x_ref, g_ref, sc_ref, sh_ref, h_ref):
    x = x_ref[...]
    y = x * lax.rsqrt(jnp.mean(x * x, axis=-1, keepdims=True) + NORM_EPS) * g_ref[...]
    h_ref[...] = (y * (1.0 + sc_ref[0]) + sh_ref[0]).astype(h_ref.dtype)


def _modnorm(x, g, scale, shift, tm, rows_per_mod):
    r, d = x.shape
    rb = scale.shape[1]
    tiles_per_mod = rows_per_mod // tm
    mod_spec = pl.BlockSpec((1, rb, d), lambda i: (i // tiles_per_mod, 0, 0))
    return pl.pallas_call(
        _modnorm_kernel,
        out_shape=jax.ShapeDtypeStruct((r, d), BF16),
        grid=(r // tm,),
        in_specs=[pl.BlockSpec((tm, d), lambda i: (i, 0)),
                  pl.BlockSpec((1, d), lambda i: (0, 0)),
                  mod_spec, mod_spec],
        out_specs=pl.BlockSpec((tm, d), lambda i: (i, 0)),
        compiler_params=_params("arbitrary"),
        name="modnorm",
    )(x, g.reshape(1, d), scale, shift)


def _inproj_kernel(h_ref, w_ref, o_ref):
    o_ref[0] = jnp.dot(h_ref[...], w_ref[...], preferred_element_type=F32)


def _inproj(h, w, tm):
    r, d = h.shape
    wsec = w.shape[1] // N_SECTIONS
    return pl.pallas_call(
        _inproj_kernel,
        out_shape=jax.ShapeDtypeStruct((N_SECTIONS, r, wsec), F32),
        grid=(r // tm, N_SECTIONS),
        in_specs=[pl.BlockSpec((tm, d), lambda i, s: (i, 0)),
                  pl.BlockSpec((d, wsec), lambda i, s: (0, s))],
        out_specs=pl.BlockSpec((1, tm, wsec), lambda i, s: (s, i, 0)),
        compiler_params=_params("arbitrary", "arbitrary"),
        name="inproj",
    )(h, w)


def _inproj_prompt_kernel(h_ref, w_ref, qkv_ref, gh_ref, kt_ref, vo_ref, *, n_heads):
    s = pl.program_id(1)
    tm = h_ref.shape[0]
    z = jnp.dot(h_ref[...], w_ref[...], preferred_element_type=F32)

    @pl.when(s <= 2)
    def _():
        qkv_ref[0] = z.astype(BF16)

    @pl.when(s == 1)
    def _():
        kt_ref[0] = z.T

    @pl.when(s == 2)
    def _():
        for hh in range(n_heads):
            vo_ref[pl.ds(hh, tm, stride=n_heads), :] = z[:, hh * HEAD_DIM:(hh + 1) * HEAD_DIM]

    @pl.when(s >= 3)
    def _():
        gh_ref[0] = z


def _inproj_prompt(h, w, batch, seq, n_heads, tm):
    r, d = h.shape
    wsec = w.shape[1] // N_SECTIONS
    tpb = seq // tm
    kern = functools.partial(_inproj_prompt_kernel, n_heads=n_heads)
    return pl.pallas_call(
        kern,
        out_shape=(jax.ShapeDtypeStruct((3, r, wsec), BF16),
                   jax.ShapeDtypeStruct((N_SECTIONS - 3, r, wsec), F32),
                   jax.ShapeDtypeStruct((batch, wsec, seq), F32),
                   jax.ShapeDtypeStruct((r * n_heads, HEAD_DIM), F32)),
        grid=(r // tm, N_SECTIONS),
        in_specs=[pl.BlockSpec((tm, d), lambda i, s: (i, 0)),
                  pl.BlockSpec((d, wsec), lambda i, s: (0, s))],
        out_specs=(pl.BlockSpec((1, tm, wsec), lambda i, s: (jnp.minimum(s, 2), i, 0)),
                   pl.BlockSpec((1, tm, wsec), lambda i, s: (jnp.maximum(s, 3) - 3, i, 0)),
                   pl.BlockSpec((1, wsec, tm), lambda i, s: (i // tpb, 0, i % tpb)),
                   pl.BlockSpec((tm * n_heads, HEAD_DIM), lambda i, s: (i, 0))),
        compiler_params=_params("arbitrary", "arbitrary"),
        name="inproj_prompt",
    )(h, w)


def _attn_prompt_kernel(par_ref, q_ref, k_ref, v_ref, g_ref, sg_ref, o_ref,
                        vt, nb_s, nbd_s, sx, sy, m_s, l_s, acc_s, *, tq, tk, out_scale):
    h = pl.program_id(1)
    qt = pl.program_id(2)
    lam = par_ref[0]
    slope2 = par_ref[1 + h] * LOG2E
    r = tq // tk
    assert r == 2

    @pl.when(qt == 0)
    def _fill():
        for j in range(k_ref.shape[1] // tk):
            vt[j] = v_ref[0, j * tk:(j + 1) * tk, :].astype(F32).T.astype(BF16)
        jj = lax.broadcasted_iota(jnp.int32, (tq, tq), 0)
        ii = lax.broadcasted_iota(jnp.int32, (tq, tq), 1)
        nb = -slope2 * (ii - jj).astype(F32)
        nb_s[...] = nb[:tk]
        nbd_s[...] = jnp.where(ii >= jj, nb, -jnp.inf)

    q = q_ref[0].astype(F32) * (QK_DIM ** -0.5 * LOG2E)
    lane = lax.broadcasted_iota(jnp.int32, q.shape, 1)
    qs = (jnp.where(lane < QK_DIM, q, 0.0).astype(BF16), jnp.where(lane >= QK_DIM, q, 0.0).astype(BF16))

    def qk(row0, n, j):
        keys = k_ref[0, pl.ds(pl.multiple_of(row0, tk), n), :]
        return lax.dot_general(keys, qs[j], _NT, preferred_element_type=F32)

    def update(j, s, kt):
        c = slope2 * (jnp.zeros((1, tq), jnp.int32) + (kt * tk - qt * tq)).astype(F32)
        m_old = m_s[j]
        m_new = jnp.maximum(m_old, jnp.max(s, axis=0, keepdims=True) + c)
        alpha = jnp.exp2(m_old - m_new)
        p = jnp.exp2(s - (m_new - c))
        m_s[j] = m_new
        l_s[j] = alpha * l_s[j] + jnp.sum(p, axis=0, keepdims=True)
        acc_s[j] = alpha * acc_s[j] + jnp.dot(vt[kt], p.astype(BF16), preferred_element_type=F32)

    for j in range(2):
        sx[j] = qk(0, tk, j)
    for j in range(2):
        s = qk(qt * tq, tq, j) + nbd_s[...]
        m = jnp.max(s, axis=0, keepdims=True)
        p = jnp.exp2(s - m)
        m_s[j] = m
        l_s[j] = jnp.sum(p, axis=0, keepdims=True)
        pb = p.astype(BF16)
        acc = jnp.dot(vt[r * qt], pb[:tk], preferred_element_type=F32)
        acc_s[j] = acc + jnp.dot(vt[r * qt + 1], pb[tk:], preferred_element_type=F32)

    def body(u, carry):
        ka = 2 * u
        kb = ka + 1
        kn = jnp.minimum(ka + 2, r * qt - 1)
        for j in range(2):
            sy[j] = qk(kb * tk, tk, j)
        for j in range(2):
            update(j, sx[j] + nb_s[...], ka)
        for j in range(2):
            sx[j] = qk(kn * tk, tk, j)
        for j in range(2):
            update(j, sy[j] + nb_s[...], kb)
        return carry

    lax.fori_loop(0, qt, body, 0)

    od = (acc_s[0] / l_s[0] - lam * (acc_s[1] / l_s[1])).T
    y = od * lax.rsqrt(jnp.mean(od * od, axis=-1, keepdims=True) + SUBLN_EPS) * sg_ref[...] * out_scale
    o_ref[...] = (y * _silu(g_ref[0])).astype(o_ref.dtype)


def _attn_prompt(qkv, gh, par, subln_g, batch, seq, n_heads, out_scale, tq=512, tk=256):
    nq = seq // tq
    kern = functools.partial(_attn_prompt_kernel, tq=tq, tk=tk, out_scale=out_scale)
    return pl.pallas_call(
        kern,
        out_shape=jax.ShapeDtypeStruct((batch * seq, n_heads * HEAD_DIM), BF16),
        grid=(batch, n_heads, nq),
        in_specs=[pl.BlockSpec(memory_space=pltpu.SMEM),
                  pl.BlockSpec((1, tq, HEAD_DIM), lambda b, h, i: (0, b * nq + i, h)),
                  pl.BlockSpec((1, seq, HEAD_DIM), lambda b, h, i: (1, b, h)),
                  pl.BlockSpec((1, seq, HEAD_DIM), lambda b, h, i: (2, b, h)),
                  pl.BlockSpec((1, tq, HEAD_DIM), lambda b, h, i: (0, b * nq + i, h)),
                  pl.BlockSpec((1, HEAD_DIM), lambda b, h, i: (0, 0))],
        out_specs=pl.BlockSpec((tq, HEAD_DIM), lambda b, h, i: (b * nq + i, h)),
        scratch_shapes=[pltpu.VMEM((seq // tk, HEAD_DIM, tk), BF16),
                        pltpu.VMEM((tk, tq), F32),
                        pltpu.VMEM((tq, tq), F32),
                        pltpu.VMEM((2, tk, tq), F32),
                        pltpu.VMEM((2, tk, tq), F32),
                        pltpu.VMEM((2, 1, tq), F32),
                        pltpu.VMEM((2, 1, tq), F32),
                        pltpu.VMEM((2, HEAD_DIM, tq), F32)],
        compiler_params=_params("arbitrary", "arbitrary", "arbitrary"),
        name="attn_prompt",
    )(par, qkv, qkv, qkv, gh, subln_g.reshape(1, HEAD_DIM))


def _attn_sample_kernel(pt_ref, par_ref, slope_ref, q_ref, kn_ref, vn_ref, g_ref, sg_ref, *rest,
                        npg, page, n_new, n_heads, out_scale):
    k_refs = rest[:npg]
    v_refs = rest[npg:2 * npg]
    o_ref = rest[2 * npg]
    qbd, kbf, vbf, b0_s, m_s, l_s, acc_s = rest[2 * npg + 1:]
    i = pl.program_id(1)
    n_steps = pl.num_programs(1)
    tblk = npg * page
    nrow, width = qbd.shape
    rows_per_head = 2 * n_new
    slope = slope_ref[...]

    def head_rows(hh):
        return slice(hh * rows_per_head, (hh + 1) * rows_per_head)

    @pl.when(i == 0)
    def _init():
        q = q_ref[0] * (QK_DIM ** -0.5)
        qt = jnp.concatenate([q] * (nrow // n_new), axis=0)
        rr = lax.broadcasted_iota(jnp.int32, (nrow, width), 0)
        cc = lax.broadcasted_iota(jnp.int32, (nrow, width), 1)
        qbd[...] = jnp.where(rr // n_new == cc // QK_DIM, qt, 0.0).astype(BF16)
        b0_s[...] = slope * lax.broadcasted_iota(jnp.int32, (nrow, tblk), 1).astype(F32)
        zpad = jnp.zeros((nrow - n_new, width), F32)
        kn = jnp.concatenate([kn_ref[0], zpad], axis=0).astype(BF16)
        vn = jnp.concatenate([vn_ref[0], zpad], axis=0).astype(BF16)
        s = lax.dot_general(qbd[...], kn, _NT, preferred_element_type=F32)
        tk = lax.broadcasted_iota(jnp.int32, (nrow, nrow), 1)
        tqry = lax.broadcasted_iota(jnp.int32, (nrow, nrow), 0) % n_new
        s = jnp.where(tk <= tqry, s + slope * tk.astype(F32), -jnp.inf)
        m = jnp.max(s, axis=1, keepdims=True)
        p = jnp.exp(s - m)
        m_s[...] = m
        l_s[...] = jnp.sum(p, axis=1, keepdims=True)
        pb = p.astype(BF16)
        for hh in range(n_heads):
            acc_s[head_rows(hh), :] = jnp.dot(pb[head_rows(hh), :], vn[:, hh * HEAD_DIM:(hh + 1) * HEAD_DIM],
                                              preferred_element_type=F32)

    for pg in range(npg):
        kbf[:, pg * page:(pg + 1) * page] = k_refs[pg][...].astype(BF16)
        for hh in range(n_heads):
            vbf[hh, pg * page:(pg + 1) * page, :] = v_refs[pg][pl.ds(hh, page, stride=n_heads), :].astype(BF16)
    s = jnp.dot(qbd[...], kbf[...], preferred_element_type=F32) + b0_s[...]
    c = slope * (jnp.zeros((nrow, 1), jnp.int32) + (i - n_steps) * tblk).astype(F32)
    m_old = m_s[...]
    m_new = jnp.maximum(m_old, jnp.max(s, axis=1, keepdims=True) + c)
    alpha = jnp.exp(m_old - m_new)
    p = jnp.exp(s - (m_new - c))
    m_s[...] = m_new
    l_s[...] = alpha * l_s[...] + jnp.sum(p, axis=1, keepdims=True)
    pb = p.astype(BF16)
    for hh in range(n_heads):
        rs = head_rows(hh)
        acc_s[rs, :] = alpha[rs, :] * acc_s[rs, :] + jnp.dot(pb[rs, :], vbf[hh], preferred_element_type=F32)

    @pl.when(i == n_steps - 1)
    def _finish():
        lam = par_ref[0]
        o = acc_s[...] / l_s[...]
        g = g_ref[0]
        for hh in range(n_heads):
            cs = slice(hh * HEAD_DIM, (hh + 1) * HEAD_DIM)
            blk = o[head_rows(hh), :]
            od = blk[:n_new] - lam * blk[n_new:]
            y = od * lax.rsqrt(jnp.mean(od * od, axis=-1, keepdims=True) + SUBLN_EPS) * sg_ref[...] * out_scale
            o_ref[0, :, cs] = y * _silu(g[:, cs])


def _attn_sample(z, cache_kt, cache_v, page_table, par, subln_g, n_heads, n_new, out_scale, npg=16):
    dec_batch, n_pages = page_table.shape
    width, page = cache_kt.shape[1], cache_kt.shape[2]
    nrow = n_heads * 2 * n_new
    assert nrow == HEAD_DIM and n_pages % npg == 0
    tblk = npg * page
    row_head = jnp.arange(nrow) // (2 * n_new)
    slope_col = jnp.exp2(-8.0 * (row_head + 1).astype(F32) / n_heads).reshape(nrow, 1)

    def page_spec(pg, rows, cols):
        return pl.BlockSpec((None, rows, cols), lambda s, i, pt: (pt[s, i * npg + pg], 0, 0))

    def sec_spec(sec):
        return pl.BlockSpec((1, None, n_new, width), lambda s, i, pt: (sec, s, 0, 0))

    kern = functools.partial(_attn_sample_kernel, npg=npg, page=page, n_new=n_new, n_heads=n_heads,
                             out_scale=out_scale)
    grid_spec = pltpu.PrefetchScalarGridSpec(
        num_scalar_prefetch=1,
        grid=(dec_batch, n_pages // npg),
        in_specs=[pl.BlockSpec(memory_space=pltpu.SMEM),
                  pl.BlockSpec((nrow, 1), lambda s, i, pt: (0, 0)),
                  sec_spec(0), sec_spec(1), sec_spec(2), sec_spec(3),
                  pl.BlockSpec((1, HEAD_DIM), lambda s, i, pt: (0, 0))]
                 + [page_spec(pg, width, page) for pg in range(npg)]
                 + [page_spec(pg, page * n_heads, HEAD_DIM) for pg in range(npg)],
        out_specs=pl.BlockSpec((1, n_new, width), lambda s, i, pt: (s, 0, 0)),
        scratch_shapes=[pltpu.VMEM((nrow, width), BF16),
                        pltpu.VMEM((width, tblk), BF16),
                        pltpu.VMEM((n_heads, tblk, HEAD_DIM), BF16),
                        pltpu.VMEM((nrow, tblk), F32),
                        pltpu.VMEM((nrow, 1), F32),
                        pltpu.VMEM((nrow, 1), F32),
                        pltpu.VMEM((nrow, HEAD_DIM), F32)],
    )
    return pl.pallas_call(
        kern,
        out_shape=jax.ShapeDtypeStruct((dec_batch, n_new, width), F32),
        grid_spec=grid_spec,
        compiler_params=_params("arbitrary", "arbitrary"),
        name="attn_sample",
    )(page_table, par, slope_col, z, z, z, z, subln_g.reshape(1, HEAD_DIM),
      *([cache_kt] * npg), *([cache_v] * npg))


def _hgrn_kernel(*refs, chunk, n_heads, has_init):
    if has_init:
        q_ref, f_ref, v_ref, g_ref, lb_ref, ng_ref, s0_ref, o_ref, sout_ref, st, b_s, k_s, qe_s, ke_s, o_s = refs
    else:
        q_ref, f_ref, v_ref, g_ref, lb_ref, ng_ref, o_ref, sout_ref, st, b_s, k_s, qe_s, ke_s, o_s = refs
        s0_ref = None
    i = pl.program_id(1)
    tb, width = f_ref.shape[1], f_ref.shape[2]
    nchunk = tb // chunk

    @pl.when(i == 0)
    def _init():
        for hh in range(n_heads):
            st[hh] = s0_ref[0, hh].T if has_init else jnp.zeros((HEAD_DIM, HEAD_DIM), F32)

    lb = lb_ref[...]
    forget = lb + (1.0 - lb) * jax.nn.sigmoid(f_ref[0])
    kk = 1.0 - forget
    b = jnp.log(forget) * LOG2E
    row = lax.broadcasted_iota(jnp.int32, b.shape, 0) % chunk
    sh = 1
    while sh < chunk:
        b = b + jnp.where(row >= sh, pltpu.roll(b, sh, axis=0), 0.0)
        sh *= 2
    b3 = b.reshape(nchunk, chunk, width)
    b_end = jnp.broadcast_to(b3[:, chunk - 1:chunk, :], b3.shape).reshape(tb, width)
    b_s[...] = b
    k_s[...] = kk
    qe_s[...] = q_ref[0] * jnp.exp2(b)
    ke_s[...] = kk * jnp.exp2(b_end - b)

    tt = lax.broadcasted_iota(jnp.int32, (chunk, chunk, HEAD_DIM), 0)
    ss = lax.broadcasted_iota(jnp.int32, (chunk, chunk, HEAD_DIM), 1)
    causal = tt >= ss

    def body(ci, carry):
        r0 = pl.multiple_of(ci * chunk, chunk)
        rs = pl.ds(r0, chunk)
        heads = [slice(hh * HEAD_DIM, (hh + 1) * HEAD_DIM) for hh in range(n_heads)]
        vbs = [v_ref[0, rs, cs].astype(BF16) for cs in heads]
        o_inter = []
        for hh, cs in enumerate(heads):
            sth = st[hh]
            o_inter.append(lax.dot_general(qe_s[rs, cs].astype(BF16), sth.astype(BF16), _NT,
                                           preferred_element_type=F32))
            dec = jnp.exp2(b_s[pl.ds(pl.multiple_of(r0 + chunk - 8, 8), 8), cs][7:8, :])
            st[hh] = sth * dec + lax.dot_general(vbs[hh], ke_s[rs, cs].astype(BF16), _TN,
                                                 preferred_element_type=F32)
        for hh, cs in enumerate(heads):
            bq = b_s[rs, cs]
            rel = jnp.where(causal, bq[:, None, :] - bq[None, :, :], -jnp.inf)
            a = jnp.sum(q_ref[0, rs, cs][:, None, :] * jnp.exp2(rel) * k_s[rs, cs][None, :, :], axis=-1)
            o_s[rs, cs] = o_inter[hh] + jnp.dot(a.astype(BF16), vbs[hh], preferred_element_type=F32)
        return carry

    lax.fori_loop(0, nchunk, body, 0)

    for hh in range(n_heads):
        cs = slice(hh * HEAD_DIM, (hh + 1) * HEAD_DIM)
        o = o_s[:, cs]
        y = o * lax.rsqrt(jnp.mean(o * o, axis=-1, keepdims=True) + NORM_EPS) * ng_ref[...]
        o_ref[0, :, cs] = (y * _silu(g_ref[0, :, cs])).astype(o_ref.dtype)

    @pl.when(i == pl.num_programs(1) - 1)
    def _emit():
        for hh in range(n_heads):
            sout_ref[0, hh] = st[hh].T


def _hgrn(z, sec0, lb, norm_g, s0, batch, length, n_heads, tb, chunk, out_dtype):
    width = z.shape[2]
    nblk = length // tb
    has_init = s0 is not None

    def sec_spec(sec):
        return pl.BlockSpec((1, tb, width), lambda b, i: (sec, b * nblk + i, 0))

    in_specs = [sec_spec(sec0), sec_spec(sec0 + 1), sec_spec(sec0 + 2), sec_spec(sec0 + 3),
                pl.BlockSpec((1, width), lambda b, i: (0, 0)),
                pl.BlockSpec((1, HEAD_DIM), lambda b, i: (0, 0))]
    args = [z, z, z, z, lb.reshape(1, width), norm_g.reshape(1, HEAD_DIM)]
    state_spec = pl.BlockSpec((1, n_heads, HEAD_DIM, HEAD_DIM), lambda b, i: (b, 0, 0, 0))
    if has_init:
        in_specs.append(state_spec)
        args.append(s0)
    kern = functools.partial(_hgrn_kernel, chunk=chunk, n_heads=n_heads, has_init=has_init)
    return pl.pallas_call(
        kern,
        out_shape=(jax.ShapeDtypeStruct((1, batch * length, width), out_dtype),
                   jax.ShapeDtypeStruct((batch, n_heads, HEAD_DIM, HEAD_DIM), F32)),
        grid=(batch, nblk),
        in_specs=in_specs,
        out_specs=(pl.BlockSpec((1, tb, width), lambda b, i: (0, b * nblk + i, 0)), state_spec),
        scratch_shapes=[pltpu.VMEM((n_heads, HEAD_DIM, HEAD_DIM), F32)]
                       + [pltpu.VMEM((tb, width), F32)] * 5,
        compiler_params=_params("arbitrary", "arbitrary"),
        name="hgrn",
    )(*args)


def _outproj_kernel(ma_ref, mh_ref, w_ref, x_ref, rg_ref, fg_ref, y_ref, *, final_norm):
    half = ma_ref.shape[1]
    u = jnp.dot(ma_ref[...].astype(BF16), w_ref[:half, :], preferred_element_type=F32)
    u = u + jnp.dot(mh_ref[...].astype(BF16), w_ref[half:, :], preferred_element_type=F32)
    r = x_ref[...] + rg_ref[0] * u
    if final_norm:
        r = r * lax.rsqrt(jnp.mean(r * r, axis=-1, keepdims=True) + NORM_EPS) * fg_ref[...]
    y_ref[...] = r


def _outproj(m_att, m_hg, w_bf, x, rg, final_g, tm, rows_per_mod, final_norm):
    r, d = x.shape
    half = m_att.shape[1]
    rb = rg.shape[1]
    tiles_per_mod = rows_per_mod // tm
    return pl.pallas_call(
        functools.partial(_outproj_kernel, final_norm=final_norm),
        out_shape=jax.ShapeDtypeStruct((r, d), F32),
        grid=(r // tm,),
        in_specs=[pl.BlockSpec((tm, half), lambda i: (i, 0)),
                  pl.BlockSpec((tm, half), lambda i: (i, 0)),
                  pl.BlockSpec((2 * half, d), lambda i: (0, 0)),
                  pl.BlockSpec((tm, d), lambda i: (i, 0)),
                  pl.BlockSpec((1, rb, d), lambda i: (i // tiles_per_mod, 0, 0)),
                  pl.BlockSpec((1, d), lambda i: (0, 0))],
        out_specs=pl.BlockSpec((tm, d), lambda i: (i, 0)),
        compiler_params=_params("arbitrary"),
        name="outproj",
    )(m_att, m_hg, w_bf, x, rg, final_g.reshape(1, d))


def kernel(x_prompt, x_sample, cache_k, cache_v, state_hgrn, page_table, c_prompt, c_sample,
           norm_g, w_ada, b_ada, w_in, lambda_q1, lambda_k1, lambda_q2, lambda_k2, subln_g,
           hg_lower_bounds, hg_norm_g, w_out, final_g):
    batch, seq, d = x_prompt.shape
    dec_batch, n_new, _ = x_sample.shape
    depth, n_pool, page, n_heads, _, qk_dim = cache_k.shape
    hg_heads = state_hgrn.shape[2]
    width = d // 2
    assert qk_dim == QK_DIM and n_heads * HEAD_DIM == width and hg_heads * HEAD_DIM == width
    assert w_in.shape[2] == N_SECTIONS * width

    slopes = jnp.asarray([2.0 ** (-8.0 * (h + 1) / n_heads) for h in range(n_heads)], F32)
    lb_all = jnp.cumsum(jax.nn.softmax(hg_lower_bounds.astype(F32), axis=0), axis=0)
    n_mod = batch + dec_batch
    n_mod_pad = -(-n_mod // 8) * 8
    c_all = jnp.concatenate([c_prompt, c_sample, jnp.zeros((n_mod_pad - n_mod, d), F32)], axis=0)

    xp = x_prompt.reshape(batch * seq, d)
    xs = x_sample.reshape(dec_batch * n_new, d)
    rs = dec_batch * n_new
    outs = [[] for _ in range(6)]
    for l in range(depth):
        last = l == depth - 1
        lam_init = 0.8 - 0.6 * math.exp(-0.3 * l)
        lam = (jnp.exp(jnp.sum(lambda_q1[l].astype(F32) * lambda_k1[l].astype(F32)))
               - jnp.exp(jnp.sum(lambda_q2[l].astype(F32) * lambda_k2[l].astype(F32))) + lam_init)
        par = jnp.concatenate([lam.reshape(1), slopes, jnp.zeros((7,), F32)])
        w_in_bf = w_in[l].astype(BF16)
        w_out_bf = w_out[l].astype(BF16)

        mod = _adaln(c_all, w_ada[l], b_ada[l])
        shift, scale, rgate = mod[:, :d], mod[:, d:2 * d], mod[:, 2 * d:]

        def rows_p(t):
            return t[:batch].reshape(batch, 1, d)

        def rows_s(t):
            return jnp.repeat(t[batch:n_mod], n_new, axis=0).reshape(1, rs, d)

        hp = _modnorm(xp, norm_g[l], rows_p(scale), rows_p(shift), tm=512, rows_per_mod=seq)
        qkv, gh, k_t, v_rows = _inproj_prompt(hp, w_in_bf, batch, seq, n_heads, tm=min(seq, 1024))
        att_p = _attn_prompt(qkv, gh, par, subln_g[l], batch, seq, n_heads, 1.0 - lam_init)
        hg_p, s_p = _hgrn(gh, 1, lb_all[l], hg_norm_g[l], None, batch, seq, hg_heads,
                          tb=256, chunk=HG_CHUNK, out_dtype=BF16)
        xp = _outproj(att_p, hg_p[0], w_out_bf, xp, rows_p(rgate), final_g,
                      tm=256, rows_per_mod=seq, final_norm=last)
        k_p = jnp.transpose(k_t.reshape(batch, n_heads, 2, QK_DIM, seq), (0, 4, 1, 2, 3))
        outs[0].append(k_p)
        outs[1].append(v_rows.reshape(batch, seq, n_heads, HEAD_DIM))
        outs[2].append(s_p)

        hs = _modnorm(xs, norm_g[l], rows_s(scale), rows_s(shift), tm=rs, rows_per_mod=rs)
        zs = _inproj(hs, w_in_bf, tm=rs)
        cache_kt = jnp.transpose(cache_k[l], (0, 2, 3, 4, 1)).reshape(n_pool, width, page)
        cache_vr = cache_v[l].reshape(n_pool, page * n_heads, HEAD_DIM)
        att_s = _attn_sample(zs.reshape(N_SECTIONS, dec_batch, n_new, width), cache_kt, cache_vr,
                             page_table, par, subln_g[l], n_heads, n_new, 1.0 - lam_init)
        hg_s, s_s = _hgrn(zs, 4, lb_all[l], hg_norm_g[l], state_hgrn[l], dec_batch, n_new, hg_heads,
                          tb=n_new, chunk=math.gcd(n_new, HG_CHUNK), out_dtype=F32)
        xs = _outproj(att_s.reshape(rs, width), hg_s[0], w_out_bf, xs, rows_s(rgate), final_g,
                      tm=rs, rows_per_mod=rs, final_norm=last)
        outs[3].append(zs[1].reshape(dec_batch, n_new, n_heads, 2, QK_DIM))
        outs[4].append(zs[2].reshape(dec_batch, n_new, n_heads, HEAD_DIM))
        outs[5].append(s_s)

    y_prompt = xp.reshape(batch, seq, d)
    y_sample = xs.reshape(dec_batch, n_new, d)
    return (y_prompt, y_sample) + tuple(jnp.stack(o) for o in outs)
```

```python
import functools
import math

import jax
import jax.numpy as jnp
from jax import lax
from jax.experimental import pallas as pl
from jax.experimental.pallas import tpu as pltpu

F32 = jnp.float32
BF16 = jnp.bfloat16

NORM_EPS = 1e-6
SUBLN_EPS = 1e-5
N_SECTIONS = 8
HEAD_DIM = 128
QK_DIM = 64
HG_CHUNK = 64
HG_BLOCK = 8
HG_GROUP = 128
LOG2E = 1.4426950408889634
VMEM_LIMIT_BYTES = 56 * 1024 * 1024

_NT = (((1,), (1,)), ((), ()))
_TN = (((0,), (0,)), ((), ()))


def _params(*sem):
    return pltpu.CompilerParams(dimension_semantics=sem, vmem_limit_bytes=VMEM_LIMIT_BYTES)


def _silu(x):
    return x * jax.nn.sigmoid(x)


def _adaln_kernel(c_ref, w_ref, b_ref, o_ref):
    a = _silu(c_ref[...]).astype(BF16)
    o_ref[...] = jnp.dot(a, w_ref[...].astype(BF16), preferred_element_type=F32) + b_ref[...]


def _adaln(c, w, b, tn=1024):
    m, d = c.shape
    n = w.shape[1]
    return pl.pallas_call(
        _adaln_kernel,
        out_shape=jax.ShapeDtypeStruct((m, n), F32),
        grid=(n // tn,),
        in_specs=[pl.BlockSpec((m, d), lambda j: (0, 0)),
                  pl.BlockSpec((d, tn), lambda j: (0, j)),
                  pl.BlockSpec((1, tn), lambda j: (0, j))],
        out_specs=pl.BlockSpec((m, tn), lambda j: (0, j)),
        compiler_params=_params("arbitrary"),
        name="adaln",
    )(c, w, b.reshape(1, n))


def _modnorm_kernel(x_ref, g_ref, sc_ref, sh_ref, h_ref):
    x = x_ref[...]
    y = x * lax.rsqrt(jnp.mean(x * x, axis=-1, keepdims=True) + NORM_EPS) * g_ref[...]
    h_ref[...] = (y * (1.0 + sc_ref[0]) + sh_ref[0]).astype(h_ref.dtype)


def _modnorm(x, g, scale, shift, tm, rows_per_mod):
    r, d = x.shape
    rb = scale.shape[1]
    tiles_per_mod = rows_per_mod // tm
    mod_spec = pl.BlockSpec((1, rb, d), lambda i: (i // tiles_per_mod, 0, 0))
    return pl.pallas_call(
        _modnorm_kernel,
        out_shape=jax.ShapeDtypeStruct((r, d), BF16),
        grid=(r // tm,),
        in_specs=[pl.BlockSpec((tm, d), lambda i: (i, 0)),
                  pl.BlockSpec((1, d), lambda i: (0, 0)),
                  mod_spec, mod_spec],
        out_specs=pl.BlockSpec((tm, d), lambda i: (i, 0)),
        compiler_params=_params("arbitrary"),
        name="modnorm",
    )(x, g.reshape(1, d), scale, shift)


def _inproj_kernel(h_ref, w_ref, o_ref):
    o_ref[0] = jnp.dot(h_ref[...], w_ref[...], preferred_element_type=F32)


def _inproj(h, w, tm):
    r, d = h.shape
    wsec = w.shape[1] // N_SECTIONS
    return pl.pallas_call(
        _inproj_kernel,
        out_shape=jax.ShapeDtypeStruct((N_SECTIONS, r, wsec), F32),
        grid=(r // tm, N_SECTIONS),
        in_specs=[pl.BlockSpec((tm, d), lambda i, s: (i, 0)),
                  pl.BlockSpec((d, wsec), lambda i, s: (0, s))],
        out_specs=pl.BlockSpec((1, tm, wsec), lambda i, s: (s, i, 0)),
        compiler_params=_params("arbitrary", "arbitrary"),
        name="inproj",
    )(h, w)


def _inproj_prompt_kernel(h_ref, w_ref, qkv_ref, gh_ref, kt_ref, vo_ref, *, n_heads):
    s = pl.program_id(1)
    tm = h_ref.shape[0]
    z = jnp.dot(h_ref[...], w_ref[...], preferred_element_type=F32)

    @pl.when(s <= 2)
    def _():
        qkv_ref[0] = z.astype(BF16)

    @pl.when(s == 1)
    def _():
        kt_ref[0] = z.T

    @pl.when(s == 2)
    def _():
        for hh in range(n_heads):
            vo_ref[pl.ds(hh, tm, stride=n_heads), :] = z[:, hh * HEAD_DIM:(hh + 1) * HEAD_DIM]

    @pl.when(s >= 3)
    def _():
        gh_ref[0] = z


def _inproj_prompt(h, w, batch, seq, n_heads, tm):
    r, d = h.shape
    wsec = w.shape[1] // N_SECTIONS
    tpb = seq // tm
    kern = functools.partial(_inproj_prompt_kernel, n_heads=n_heads)
    return pl.pallas_call(
        kern,
        out_shape=(jax.ShapeDtypeStruct((3, r, wsec), BF16),
                   jax.ShapeDtypeStruct((N_SECTIONS - 3, r, wsec), F32),
                   jax.ShapeDtypeStruct((batch, wsec, seq), F32),
                   jax.ShapeDtypeStruct((r * n_heads, HEAD_DIM), F32)),
        grid=(r // tm, N_SECTIONS),
        in_specs=[pl.BlockSpec((tm, d), lambda i, s: (i, 0)),
                  pl.BlockSpec((d, wsec), lambda i, s: (0, s))],
        out_specs=(pl.BlockSpec((1, tm, wsec), lambda i, s: (jnp.minimum(s, 2), i, 0)),
                   pl.BlockSpec((1, tm, wsec), lambda i, s: (jnp.maximum(s, 3) - 3, i, 0)),
                   pl.BlockSpec((1, wsec, tm), lambda i, s: (i // tpb, 0, i % tpb)),
                   pl.BlockSpec((tm * n_heads, HEAD_DIM), lambda i, s: (i, 0))),
        compiler_params=_params("arbitrary", "arbitrary"),
        name="inproj_prompt",
    )(h, w)


def _attn_prompt_kernel(par_ref, q_ref, k_ref, v_ref, g_ref, sg_ref, o_ref,
                        vt, kx_s, mask_s, sx, sy, m_s, acc_s, *, tq, tk, out_scale):
    h = pl.program_id(1)
    qt = pl.program_id(2)
    lam = par_ref[0]
    slope2 = par_ref[1 + h] * LOG2E
    assert tq == 2 * tk

    @pl.when(qt == 0)
    def _fill():
        ones_row = (lax.broadcasted_iota(jnp.int32, (vt.shape[1] - HEAD_DIM, tk), 0) == 0).astype(BF16)
        for j in range(k_ref.shape[1] // tk):
            vt[j, :HEAD_DIM, :] = v_ref[0, j * tk:(j + 1) * tk, :].astype(F32).T.astype(BF16)
            vt[j, HEAD_DIM:, :] = ones_row
        jl = lax.broadcasted_iota(jnp.int32, (tk, HEAD_DIM), 0).astype(F32) * slope2
        hi = jl.astype(BF16).astype(F32)
        mid = (jl - hi).astype(BF16).astype(F32)
        lo = jl - hi - mid
        lane = lax.broadcasted_iota(jnp.int32, (tk, HEAD_DIM), 1)
        kx_s[...] = jnp.where(lane == 0, hi, jnp.where(lane == 1, mid, jnp.where(lane == 2, lo, 0.0))).astype(BF16)
        jj = lax.broadcasted_iota(jnp.int32, (tk, tq), 0)
        ii = lax.broadcasted_iota(jnp.int32, (tk, tq), 1)
        mask_s[0] = jnp.where((ii >= tk) | (ii >= jj), 0.0, -jnp.inf)
        mask_s[1] = jnp.where(ii - tk >= jj, 0.0, -jnp.inf)

    q = q_ref[0].astype(F32) * (QK_DIM ** -0.5 * LOG2E)
    lane = lax.broadcasted_iota(jnp.int32, q.shape, 1)
    qx = (lane < 3).astype(F32).astype(BF16)
    qs = (jnp.concatenate([jnp.where(lane < QK_DIM, q, 0.0).astype(BF16), qx], axis=1),
          jnp.concatenate([jnp.where(lane >= QK_DIM, q, 0.0).astype(BF16), qx], axis=1))
    c0 = -slope2 * lax.broadcasted_iota(jnp.int32, (1, tq), 1).astype(F32)

    def qk(kt, j):
        keys = k_ref[0, pl.ds(pl.multiple_of(kt * tk, tk), tk), :]
        return lax.dot_general(jnp.concatenate([keys, kx_s[...]], axis=1), qs[j], _NT,
                               preferred_element_type=F32)

    def update(j, s, kt):
        c = c0 + slope2 * (jnp.zeros((1, tq), jnp.int32) + (kt * tk - qt * tq)).astype(F32)
        m_old = m_s[j]
        m_new = jnp.maximum(m_old, jnp.max(s, axis=0, keepdims=True) + c)
        alpha = jnp.exp2(m_old - m_new)
        p = jnp.exp2(s - (m_new - c))
        m_s[j] = m_new
        acc_s[j] = alpha * acc_s[j] + jnp.dot(vt[kt], p.astype(BF16), preferred_element_type=F32)

    m_s[...] = jnp.full(m_s.shape, -jnp.inf, F32)
    acc_s[...] = jnp.zeros(acc_s.shape, F32)
    for j in range(2):
        sx[j] = qk(0, j)

    def body(u, carry):
        ka = 2 * u
        for j in range(2):
            sy[j] = qk(ka + 1, j)
        for j in range(2):
            update(j, sx[j], ka)
        for j in range(2):
            sx[j] = qk(ka + 2, j)
        for j in range(2):
            update(j, sy[j], ka + 1)
        return carry

    lax.fori_loop(0, qt, body, 0)

    kd = 2 * qt
    for j in range(2):
        sy[j] = qk(kd + 1, j)
    for j in range(2):
        update(j, sx[j] + mask_s[0], kd)
    for j in range(2):
        update(j, sy[j] + mask_s[1], kd + 1)

    o0 = acc_s[0]
    o1 = acc_s[1]
    od = (o0[:HEAD_DIM] / o0[HEAD_DIM:HEAD_DIM + 1] - lam * (o1[:HEAD_DIM] / o1[HEAD_DIM:HEAD_DIM + 1])).T
    y = od * lax.rsqrt(jnp.mean(od * od, axis=-1, keepdims=True) + SUBLN_EPS) * sg_ref[...] * out_scale
    o_ref[...] = (y * _silu(g_ref[0])).astype(o_ref.dtype)


def _attn_prompt(qkv, gh, par, subln_g, batch, seq, n_heads, out_scale, tq=512, tk=256):
    nq = seq // tq
    kern = functools.partial(_attn_prompt_kernel, tq=tq, tk=tk, out_scale=out_scale)
    return pl.pallas_call(
        kern,
        out_shape=jax.ShapeDtypeStruct((batch * seq, n_heads * HEAD_DIM), BF16),
        grid=(batch, n_heads, nq),
        in_specs=[pl.BlockSpec(memory_space=pltpu.SMEM),
                  pl.BlockSpec((1, tq, HEAD_DIM), lambda b, h, i: (0, b * nq + i, h)),
                  pl.BlockSpec((1, seq, HEAD_DIM), lambda b, h, i: (1, b, h)),
                  pl.BlockSpec((1, seq, HEAD_DIM), lambda b, h, i: (2, b, h)),
                  pl.BlockSpec((1, tq, HEAD_DIM), lambda b, h, i: (0, b * nq + i, h)),
                  pl.BlockSpec((1, HEAD_DIM), lambda b, h, i: (0, 0))],
        out_specs=pl.BlockSpec((tq, HEAD_DIM), lambda b, h, i: (b * nq + i, h)),
        scratch_shapes=[pltpu.VMEM((seq // tk, HEAD_DIM + 16, tk), BF16),
                        pltpu.VMEM((tk, HEAD_DIM), BF16),
                        pltpu.VMEM((2, tk, tq), F32),
                        pltpu.VMEM((2, tk, tq), F32),
                        pltpu.VMEM((2, tk, tq), F32),
                        pltpu.VMEM((2, 1, tq), F32),
                        pltpu.VMEM((2, HEAD_DIM + 16, tq), F32)],
        compiler_params=_params("arbitrary", "arbitrary", "arbitrary"),
        name="attn_prompt",
    )(par, qkv, qkv, qkv, gh, subln_g.reshape(1, HEAD_DIM))


def _attn_sample_kernel(pt_ref, par_ref, slope_ref, q_ref, kn_ref, vn_ref, g_ref, sg_ref, *rest,
                        npg, page, n_new, n_heads, out_scale):
    k_refs = rest[:npg]
    v_refs = rest[npg:2 * npg]
    o_ref = rest[2 * npg]
    qbd, kbf, vbf, b0_s, m_s, l_s, acc_s = rest[2 * npg + 1:]
    i = pl.program_id(1)
    n_steps = pl.num_programs(1)
    tblk = npg * page
    nrow, width = qbd.shape
    rows_per_head = 2 * n_new
    slope = slope_ref[...]

    def head_rows(hh):
        return slice(hh * rows_per_head, (hh + 1) * rows_per_head)

    @pl.when(i == 0)
    def _init():
        q = q_ref[0] * (QK_DIM ** -0.5)
        qt = jnp.concatenate([q] * (nrow // n_new), axis=0)
        rr = lax.broadcasted_iota(jnp.int32, (nrow, width), 0)
        cc = lax.broadcasted_iota(jnp.int32, (nrow, width), 1)
        qbd[...] = jnp.where(rr // n_new == cc // QK_DIM, qt, 0.0).astype(BF16)
        b0_s[...] = slope * lax.broadcasted_iota(jnp.int32, (nrow, tblk), 1).astype(F32)
        zpad = jnp.zeros((nrow - n_new, width), F32)
        kn = jnp.concatenate([kn_ref[0], zpad], axis=0).astype(BF16)
        vn = jnp.concatenate([vn_ref[0], zpad], axis=0).astype(BF16)
        s = lax.dot_general(qbd[...], kn, _NT, preferred_element_type=F32)
        tk = lax.broadcasted_iota(jnp.int32, (nrow, nrow), 1)
        tqry = lax.broadcasted_iota(jnp.int32, (nrow, nrow), 0) % n_new
        s = jnp.where(tk <= tqry, s + slope * tk.astype(F32), -jnp.inf)
        m = jnp.max(s, axis=1, keepdims=True)
        p = jnp.exp(s - m)
        m_s[...] = m
        l_s[...] = jnp.sum(p, axis=1, keepdims=True)
        pb = p.astype(BF16)
        for hh in range(n_heads):
            acc_s[head_rows(hh), :] = jnp.dot(pb[head_rows(hh), :], vn[:, hh * HEAD_DIM:(hh + 1) * HEAD_DIM],
                                              preferred_element_type=F32)

    for pg in range(npg):
        kbf[:, pg * page:(pg + 1) * page] = k_refs[pg][...].astype(BF16)
        for hh in range(n_heads):
            vbf[hh, pg * page:(pg + 1) * page, :] = v_refs[pg][pl.ds(hh, page, stride=n_heads), :].astype(BF16)
    s = jnp.dot(qbd[...], kbf[...], preferred_element_type=F32) + b0_s[...]
    c = slope * (jnp.zeros((nrow, 1), jnp.int32) + (i - n_steps) * tblk).astype(F32)
    m_old = m_s[...]
    m_new = jnp.maximum(m_old, jnp.max(s, axis=1, keepdims=True) + c)
    alpha = jnp.exp(m_old - m_new)
    p = jnp.exp(s - (m_new - c))
    m_s[...] = m_new
    l_s[...] = alpha * l_s[...] + jnp.sum(p, axis=1, keepdims=True)
    pb = p.astype(BF16)
    for hh in range(n_heads):
        rs = head_rows(hh)
        acc_s[rs, :] = alpha[rs, :] * acc_s[rs, :] + jnp.dot(pb[rs, :], vbf[hh], preferred_element_type=F32)

    @pl.when(i == n_steps - 1)
    def _finish():
        lam = par_ref[0]
        o = acc_s[...] / l_s[...]
        g = g_ref[0]
        for hh in range(n_heads):
            cs = slice(hh * HEAD_DIM, (hh + 1) * HEAD_DIM)
            blk = o[head_rows(hh), :]
            od = blk[:n_new] - lam * blk[n_new:]
            y = od * lax.rsqrt(jnp.mean(od * od, axis=-1, keepdims=True) + SUBLN_EPS) * sg_ref[...] * out_scale
            o_ref[0, :, cs] = y * _silu(g[:, cs])


def _attn_sample(z, cache_kt, cache_v, page_table, par, subln_g, n_heads, n_new, out_scale, npg=16):
    dec_batch, n_pages = page_table.shape
    width, page = cache_kt.shape[1], cache_kt.shape[2]
    nrow = n_heads * 2 * n_new
    assert nrow == HEAD_DIM and n_pages % npg == 0
    tblk = npg * page
    row_head = jnp.arange(nrow) // (2 * n_new)
    slope_col = jnp.exp2(-8.0 * (row_head + 1).astype(F32) / n_heads).reshape(nrow, 1)

    def page_spec(pg, rows, cols):
        return pl.BlockSpec((None, rows, cols), lambda s, i, pt: (pt[s, i * npg + pg], 0, 0))

    def sec_spec(sec):
        return pl.BlockSpec((1, None, n_new, width), lambda s, i, pt: (sec, s, 0, 0))

    kern = functools.partial(_attn_sample_kernel, npg=npg, page=page, n_new=n_new, n_heads=n_heads,
                             out_scale=out_scale)
    grid_spec = pltpu.PrefetchScalarGridSpec(
        num_scalar_prefetch=1,
        grid=(dec_batch, n_pages // npg),
        in_specs=[pl.BlockSpec(memory_space=pltpu.SMEM),
                  pl.BlockSpec((nrow, 1), lambda s, i, pt: (0, 0)),
                  sec_spec(0), sec_spec(1), sec_spec(2), sec_spec(3),
                  pl.BlockSpec((1, HEAD_DIM), lambda s, i, pt: (0, 0))]
                 + [page_spec(pg, width, page) for pg in range(npg)]
                 + [page_spec(pg, page * n_heads, HEAD_DIM) for pg in range(npg)],
        out_specs=pl.BlockSpec((1, n_new, width), lambda s, i, pt: (s, 0, 0)),
        scratch_shapes=[pltpu.VMEM((nrow, width), BF16),
                        pltpu.VMEM((width, tblk), BF16),
                        pltpu.VMEM((n_heads, tblk, HEAD_DIM), BF16),
                        pltpu.VMEM((nrow, tblk), F32),
                        pltpu.VMEM((nrow, 1), F32),
                        pltpu.VMEM((nrow, 1), F32),
                        pltpu.VMEM((nrow, HEAD_DIM), F32)],
    )
    return pl.pallas_call(
        kern,
        out_shape=jax.ShapeDtypeStruct((dec_batch, n_new, width), F32),
        grid_spec=grid_spec,
        compiler_params=_params("arbitrary", "arbitrary"),
        name="attn_sample",
    )(page_table, par, slope_col, z, z, z, z, subln_g.reshape(1, HEAD_DIM),
      *([cache_kt] * npg), *([cache_v] * npg))


def _hgrn_kernel(*refs, chunk, n_heads, has_init):
    if has_init:
        (q_ref, f_ref, v_ref, g_ref, lb_ref, ng_ref, s0_ref, o_ref, sout_ref,
         st, b_s, k_s, qe_s, ke_s, o_s, u_s, stc_s) = refs
    else:
        (q_ref, f_ref, v_ref, g_ref, lb_ref, ng_ref, o_ref, sout_ref,
         st, b_s, k_s, qe_s, ke_s, o_s, u_s, stc_s) = refs
        s0_ref = None
    i = pl.program_id(2)
    tb, width = f_ref.shape[1], f_ref.shape[2]
    nchunk = tb // chunk
    grp = min(tb, HG_GROUP)
    levels = [w for w in (32, 16, 8) if 2 * w <= chunk]
    assert chunk % HG_BLOCK == 0 and grp % chunk == 0 and tb % grp == 0
    heads = [slice(hh * HEAD_DIM, (hh + 1) * HEAD_DIM) for hh in range(n_heads)]

    @pl.when(i == 0)
    def _init():
        for hh in range(n_heads):
            st[hh] = s0_ref[0, hh].T if has_init else jnp.zeros((HEAD_DIM, HEAD_DIM), F32)

    lb = lb_ref[...]
    forget = lb + (1.0 - lb) * jax.nn.sigmoid(f_ref[0])
    kk = 1.0 - forget
    b = jnp.log(forget) * LOG2E
    row = lax.broadcasted_iota(jnp.int32, b.shape, 0) % chunk
    sh = 1
    while sh < chunk:
        b = b + jnp.where(row >= sh, pltpu.roll(b, sh, axis=0), 0.0)
        sh *= 2
    b3 = b.reshape(nchunk, chunk, width)
    b_end = jnp.broadcast_to(b3[:, chunk - 1:chunk, :], b3.shape).reshape(tb, width)
    b_s[...] = b
    k_s[...] = kk
    qe_s[...] = q_ref[0] * jnp.exp2(b)
    ke_s[...] = kk * jnp.exp2(b_end - b)

    rowi = lax.broadcasted_iota(jnp.int32, (grp, HEAD_DIM), 0)
    ti = lax.broadcasted_iota(jnp.int32, (grp, grp), 0)
    si = lax.broadcasted_iota(jnp.int32, (grp, grp), 1)

    def block_row(x, period, r):
        x3 = x.reshape(grp // period, period, HEAD_DIM)
        return jnp.broadcast_to(x3[:, r:r + 1, :], x3.shape).reshape(grp, HEAD_DIM)

    def neg_unless(cond):
        return jnp.where(cond, 0.0, -jnp.inf)

    same_block = ti // HG_BLOCK == si // HG_BLOCK
    row_from = [neg_unless(rowi % HG_BLOCK >= s) for s in range(HG_BLOCK)]
    pair_at = [(same_block & (si % HG_BLOCK == s)).astype(F32) for s in range(HG_BLOCK)]
    split = [(neg_unless(rowi % (2 * w) >= w), neg_unless(rowi % (2 * w) < w),
              (ti // (2 * w) == si // (2 * w)).astype(F32)) for w in levels]

    def group_body(gi, carry):
        rs = pl.ds(pl.multiple_of(gi * grp, grp), grp)
        for cs in heads:
            qg = q_ref[0, rs, cs]
            kg = k_s[rs, cs]
            bg = b_s[rs, cs]
            kgb = kg.astype(BF16)
            a = jnp.zeros((grp, grp), F32)
            for s in range(HG_BLOCK):
                e = jnp.exp2(bg - block_row(bg, HG_BLOCK, s) + row_from[s])
                x = lax.dot_general((qg * e).astype(BF16), kgb, _NT, preferred_element_type=F32)
                a = a + x * pair_at[s]
            for w, (later, earlier, same_parent) in zip(levels, split):
                d = bg - block_row(bg, 2 * w, w)
                x = lax.dot_general((qg * jnp.exp2(d + later)).astype(BF16),
                                    (kg * jnp.exp2(earlier - d)).astype(BF16), _NT,
                                    preferred_element_type=F32)
                a = a + x * same_parent
            o_s[rs, cs] = jnp.dot(a.astype(BF16), v_ref[0, rs, cs].astype(BF16), preferred_element_type=F32)
        return carry

    lax.fori_loop(0, tb // grp, group_body, 0, unroll=2 if (tb // grp) % 2 == 0 else 1)

    for hh, cs in enumerate(heads):
        for c in range(nchunk):
            rs = slice(c * chunk, (c + 1) * chunk)
            u_s[c] = lax.dot_general(v_ref[0, rs, cs].astype(BF16), ke_s[rs, cs].astype(BF16), _TN,
                                     preferred_element_type=F32)
        s_cur = st[hh]
        for c in range(nchunk):
            stc_s[c] = s_cur.astype(BF16)
            last = (c + 1) * chunk - 1
            s_cur = s_cur * jnp.exp2(b_s[last:last + 1, cs]) + u_s[c]
        st[hh] = s_cur
        for c in range(nchunk):
            rs = slice(c * chunk, (c + 1) * chunk)
            o_s[rs, cs] += lax.dot_general(qe_s[rs, cs].astype(BF16), stc_s[c], _NT,
                                           preferred_element_type=F32)

    for cs in heads:
        o = o_s[:, cs]
        y = o * lax.rsqrt(jnp.mean(o * o, axis=-1, keepdims=True) + NORM_EPS) * ng_ref[...]
        o_ref[0, :, cs] = (y * _silu(g_ref[0, :, cs])).astype(o_ref.dtype)

    @pl.when(i == pl.num_programs(2) - 1)
    def _emit():
        for hh in range(n_heads):
            sout_ref[0, hh] = st[hh].T


def _hgrn(z, sec0, lb, norm_g, s0, batch, length, n_heads, heads_per_step, tb, chunk, out_dtype):
    width = z.shape[2]
    nblk = length // tb
    wblk = heads_per_step * HEAD_DIM
    has_init = s0 is not None

    def sec_spec(sec):
        return pl.BlockSpec((1, tb, wblk), lambda b, h, i: (sec, b * nblk + i, h))

    in_specs = [sec_spec(sec0), sec_spec(sec0 + 1), sec_spec(sec0 + 2), sec_spec(sec0 + 3),
                pl.BlockSpec((1, wblk), lambda b, h, i: (0, h)),
                pl.BlockSpec((1, HEAD_DIM), lambda b, h, i: (0, 0))]
    args = [z, z, z, z, lb.reshape(1, width), norm_g.reshape(1, HEAD_DIM)]
    state_spec = pl.BlockSpec((1, heads_per_step, HEAD_DIM, HEAD_DIM), lambda b, h, i: (b, h, 0, 0))
    if has_init:
        in_specs.append(state_spec)
        args.append(s0)
    kern = functools.partial(_hgrn_kernel, chunk=chunk, n_heads=heads_per_step, has_init=has_init)
    return pl.pallas_call(
        kern,
        out_shape=(jax.ShapeDtypeStruct((1, batch * length, width), out_dtype),
                   jax.ShapeDtypeStruct((batch, n_heads, HEAD_DIM, HEAD_DIM), F32)),
        grid=(batch, n_heads // heads_per_step, nblk),
        in_specs=in_specs,
        out_specs=(pl.BlockSpec((1, tb, wblk), lambda b, h, i: (0, b * nblk + i, h)), state_spec),
        scratch_shapes=[pltpu.VMEM((heads_per_step, HEAD_DIM, HEAD_DIM), F32)]
                       + [pltpu.VMEM((tb, wblk), F32)] * 5
                       + [pltpu.VMEM((tb // chunk, HEAD_DIM, HEAD_DIM), F32),
                          pltpu.VMEM((tb // chunk, HEAD_DIM, HEAD_DIM), BF16)],
        compiler_params=_params("arbitrary", "arbitrary", "arbitrary"),
        name="hgrn",
    )(*args)


def _outproj_kernel(ma_ref, mh_ref, w_ref, x_ref, rg_ref, fg_ref, y_ref, *, final_norm):
    half = ma_ref.shape[1]
    u = jnp.dot(ma_ref[...].astype(BF16), w_ref[:half, :], preferred_element_type=F32)
    u = u + jnp.dot(mh_ref[...].astype(BF16), w_ref[half:, :], preferred_element_type=F32)
    r = x_ref[...] + rg_ref[0] * u
    if final_norm:
        r = r * lax.rsqrt(jnp.mean(r * r, axis=-1, keepdims=True) + NORM_EPS) * fg_ref[...]
    y_ref[...] = r


def _outproj(m_att, m_hg, w_bf, x, rg, final_g, tm, rows_per_mod, final_norm):
    r, d = x.shape
    half = m_att.shape[1]
    rb = rg.shape[1]
    tiles_per_mod = rows_per_mod // tm
    return pl.pallas_call(
        functools.partial(_outproj_kernel, final_norm=final_norm),
        out_shape=jax.ShapeDtypeStruct((r, d), F32),
        grid=(r // tm,),
        in_specs=[pl.BlockSpec((tm, half), lambda i: (i, 0)),
                  pl.BlockSpec((tm, half), lambda i: (i, 0)),
                  pl.BlockSpec((2 * half, d), lambda i: (0, 0)),
                  pl.BlockSpec((tm, d), lambda i: (i, 0)),
                  pl.BlockSpec((1, rb, d), lambda i: (i // tiles_per_mod, 0, 0)),
                  pl.BlockSpec((1, d), lambda i: (0, 0))],
        out_specs=pl.BlockSpec((tm, d), lambda i: (i, 0)),
        compiler_params=_params("arbitrary"),
        name="outproj",
    )(m_att, m_hg, w_bf, x, rg, final_g.reshape(1, d))


def kernel(x_prompt, x_sample, cache_k, cache_v, state_hgrn, page_table, c_prompt, c_sample,
           norm_g, w_ada, b_ada, w_in, lambda_q1, lambda_k1, lambda_q2, lambda_k2, subln_g,
           hg_lower_bounds, hg_norm_g, w_out, final_g):
    batch, seq, d = x_prompt.shape
    dec_batch, n_new, _ = x_sample.shape
    depth, n_pool, page, n_heads, _, qk_dim = cache_k.shape
    hg_heads = state_hgrn.shape[2]
    width = d // 2
    assert qk_dim == QK_DIM and n_heads * HEAD_DIM == width and hg_heads * HEAD_DIM == width
    assert w_in.shape[2] == N_SECTIONS * width

    slopes = jnp.asarray([2.0 ** (-8.0 * (h + 1) / n_heads) for h in range(n_heads)], F32)
    lb_all = jnp.cumsum(jax.nn.softmax(hg_lower_bounds.astype(F32), axis=0), axis=0)
    n_mod = batch + dec_batch
    n_mod_pad = -(-n_mod // 8) * 8
    c_all = jnp.concatenate([c_prompt, c_sample, jnp.zeros((n_mod_pad - n_mod, d), F32)], axis=0)

    xp = x_prompt.reshape(batch * seq, d)
    xs = x_sample.reshape(dec_batch * n_new, d)
    rs = dec_batch * n_new
    outs = [[] for _ in range(6)]
    for l in range(depth):
        last = l == depth - 1
        lam_init = 0.8 - 0.6 * math.exp(-0.3 * l)
        lam = (jnp.exp(jnp.sum(lambda_q1[l].astype(F32) * lambda_k1[l].astype(F32)))
               - jnp.exp(jnp.sum(lambda_q2[l].astype(F32) * lambda_k2[l].astype(F32))) + lam_init)
        par = jnp.concatenate([lam.reshape(1), slopes, jnp.zeros((7,), F32)])
        w_in_bf = w_in[l].astype(BF16)
        w_out_bf = w_out[l].astype(BF16)

        mod = _adaln(c_all, w_ada[l], b_ada[l])
        shift, scale, rgate = mod[:, :d], mod[:, d:2 * d], mod[:, 2 * d:]

        def rows_p(t):
            return t[:batch].reshape(batch, 1, d)

        def rows_s(t):
            return jnp.repeat(t[batch:n_mod], n_new, axis=0).reshape(1, rs, d)

        hp = _modnorm(xp, norm_g[l], rows_p(scale), rows_p(shift), tm=512, rows_per_mod=seq)
        qkv, gh, k_t, v_rows = _inproj_prompt(hp, w_in_bf, batch, seq, n_heads, tm=min(seq, 1024))
        att_p = _attn_prompt(qkv, gh, par, subln_g[l], batch, seq, n_heads, 1.0 - lam_init)
        hg_p, s_p = _hgrn(gh, 1, lb_all[l], hg_norm_g[l], None, batch, seq, hg_heads, heads_per_step=1,
                          tb=min(seq, 1024), chunk=math.gcd(seq, HG_CHUNK), out_dtype=BF16)
        xp = _outproj(att_p, hg_p[0], w_out_bf, xp, rows_p(rgate), final_g,
                      tm=256, rows_per_mod=seq, final_norm=last)
        k_p = jnp.transpose(k_t.reshape(batch, n_heads, 2, QK_DIM, seq), (0, 4, 1, 2, 3))
        outs[0].append(k_p)
        outs[1].append(v_rows.reshape(batch, seq, n_heads, HEAD_DIM))
        outs[2].append(s_p)

        hs = _modnorm(xs, norm_g[l], rows_s(scale), rows_s(shift), tm=rs, rows_per_mod=rs)
        zs = _inproj(hs, w_in_bf, tm=rs)
        cache_kt = jnp.transpose(cache_k[l], (0, 2, 3, 4, 1)).reshape(n_pool, width, page)
        cache_vr = cache_v[l].reshape(n_pool, page * n_heads, HEAD_DIM)
        att_s = _attn_sample(zs.reshape(N_SECTIONS, dec_batch, n_new, width), cache_kt, cache_vr,
                             page_table, par, subln_g[l], n_heads, n_new, 1.0 - lam_init)
        hg_s, s_s = _hgrn(zs, 4, lb_all[l], hg_norm_g[l], state_hgrn[l], dec_batch, n_new, hg_heads,
                          heads_per_step=hg_heads, tb=n_new, chunk=math.gcd(n_new, HG_CHUNK), out_dtype=F32)
        xs = _outproj(att_s.reshape(rs, width), hg_s[0], w_out_bf, xs, rows_s(rgate), final_g,
                      tm=rs, rows_per_mod=rs, final_norm=last)
        outs[3].append(zs[1].reshape(dec_batch, n_new, n_heads, 2, QK_DIM))
        outs[4].append(zs[2].reshape(dec_batch, n_new, n_heads, HEAD_DIM))
        outs[5].append(s_s)

    y_prompt = xp.reshape(batch, seq, d)
    y_sample = xs.reshape(dec_batch, n_new, d)
    return (y_prompt, y_sample) + tuple(jnp.stack(o) for o in outs)
```

```python
import functools
import math

import jax
import jax.numpy as jnp
from jax import lax
from jax.experimental import pallas as pl
from jax.experimental.pallas import tpu as pltpu

F32 = jnp.float32
BF16 = jnp.bfloat16

NORM_EPS = 1e-6
SUBLN_EPS = 1e-5
N_SECTIONS = 8
HEAD_DIM = 128
QK_DIM = 64
HG_CHUNK = 64
HG_BLOCK = 8
HG_GROUP = 128
LOG2E = 1.4426950408889634
VMEM_LIMIT_BYTES = 56 * 1024 * 1024

_NT = (((1,), (1,)), ((), ()))
_TN = (((0,), (0,)), ((), ()))


def _params(*sem):
    return pltpu.CompilerParams(dimension_semantics=sem, vmem_limit_bytes=VMEM_LIMIT_BYTES)


def _silu(x):
    return x * jax.nn.sigmoid(x)


def _adaln_kernel(c_ref, w_ref, b_ref, o_ref):
    a = _silu(c_ref[...]).astype(BF16)
    o_ref[...] = jnp.dot(a, w_ref[...].astype(BF16), preferred_element_type=F32) + b_ref[...]


def _adaln(c, w, b, tn=1024):
    m, d = c.shape
    n = w.shape[1]
    return pl.pallas_call(
        _adaln_kernel,
        out_shape=jax.ShapeDtypeStruct((m, n), F32),
        grid=(n // tn,),
        in_specs=[pl.BlockSpec((m, d), lambda j: (0, 0)),
                  pl.BlockSpec((d, tn), lambda j: (0, j)),
                  pl.BlockSpec((1, tn), lambda j: (0, j))],
        out_specs=pl.BlockSpec((m, tn), lambda j: (0, j)),
        compiler_params=_params("arbitrary"),
        name="adaln",
    )(c, w, b.reshape(1, n))


def _modnorm_kernel(x_ref, g_ref, sc_ref, sh_ref, h_ref):
    x = x_ref[...]
    y = x * lax.rsqrt(jnp.mean(x * x, axis=-1, keepdims=True) + NORM_EPS) * g_ref[...]
    h_ref[...] = (y * (1.0 + sc_ref[0]) + sh_ref[0]).astype(h_ref.dtype)


def _modnorm(x, g, scale, shift, tm, rows_per_mod):
    r, d = x.shape
    rb = scale.shape[1]
    tiles_per_mod = rows_per_mod // tm
    mod_spec = pl.BlockSpec((1, rb, d), lambda i: (i // tiles_per_mod, 0, 0))
    return pl.pallas_call(
        _modnorm_kernel,
        out_shape=jax.ShapeDtypeStruct((r, d), BF16),
        grid=(r // tm,),
        in_specs=[pl.BlockSpec((tm, d), lambda i: (i, 0)),
                  pl.BlockSpec((1, d), lambda i: (0, 0)),
                  mod_spec, mod_spec],
        out_specs=pl.BlockSpec((tm, d), lambda i: (i, 0)),
        compiler_params=_params("arbitrary"),
        name="modnorm",
    )(x, g.reshape(1, d), scale, shift)


def _inproj_kernel(h_ref, w_ref, o_ref):
    o_ref[0] = jnp.dot(h_ref[...], w_ref[...], preferred_element_type=F32)


def _inproj(h, w, tm):
    r, d = h.shape
    wsec = w.shape[1] // N_SECTIONS
    return pl.pallas_call(
        _inproj_kernel,
        out_shape=jax.ShapeDtypeStruct((N_SECTIONS, r, wsec), F32),
        grid=(r // tm, N_SECTIONS),
        in_specs=[pl.BlockSpec((tm, d), lambda i, s: (i, 0)),
                  pl.BlockSpec((d, wsec), lambda i, s: (0, s))],
        out_specs=pl.BlockSpec((1, tm, wsec), lambda i, s: (s, i, 0)),
        compiler_params=_params("arbitrary", "arbitrary"),
        name="inproj",
    )(h, w)


def _inproj_prompt_kernel(h_ref, w_ref, qkv_ref, gh_ref, kt_ref, vo_ref, *, n_heads):
    s = pl.program_id(1)
    tm = h_ref.shape[0]
    z = jnp.dot(h_ref[...], w_ref[...], preferred_element_type=F32)

    @pl.when(s <= 2)
    def _():
        qkv_ref[0] = z.astype(BF16)

    @pl.when(s == 1)
    def _():
        kt_ref[0] = z.T

    @pl.when(s == 2)
    def _():
        for hh in range(n_heads):
            vo_ref[pl.ds(hh, tm, stride=n_heads), :] = z[:, hh * HEAD_DIM:(hh + 1) * HEAD_DIM]

    @pl.when(s >= 3)
    def _():
        gh_ref[0] = z


def _inproj_prompt(h, w, batch, seq, n_heads, tm):
    r, d = h.shape
    wsec = w.shape[1] // N_SECTIONS
    tpb = seq // tm
    kern = functools.partial(_inproj_prompt_kernel, n_heads=n_heads)
    return pl.pallas_call(
        kern,
        out_shape=(jax.ShapeDtypeStruct((3, r, wsec), BF16),
                   jax.ShapeDtypeStruct((N_SECTIONS - 3, r, wsec), F32),
                   jax.ShapeDtypeStruct((batch, wsec, seq), F32),
                   jax.ShapeDtypeStruct((r * n_heads, HEAD_DIM), F32)),
        grid=(r // tm, N_SECTIONS),
        in_specs=[pl.BlockSpec((tm, d), lambda i, s: (i, 0)),
                  pl.BlockSpec((d, wsec), lambda i, s: (0, s))],
        out_specs=(pl.BlockSpec((1, tm, wsec), lambda i, s: (jnp.minimum(s, 2), i, 0)),
                   pl.BlockSpec((1, tm, wsec), lambda i, s: (jnp.maximum(s, 3) - 3, i, 0)),
                   pl.BlockSpec((1, wsec, tm), lambda i, s: (i // tpb, 0, i % tpb)),
                   pl.BlockSpec((tm * n_heads, HEAD_DIM), lambda i, s: (i, 0))),
        compiler_params=_params("arbitrary", "arbitrary"),
        name="inproj_prompt",
    )(h, w)


def _attn_prompt_kernel(par_ref, q_ref, k_ref, v_ref, g_ref, sg_ref, o_ref,
                        vt, kx_s, mask_s, sx, sy, m_s, acc_s, *, tq, tk, out_scale):
    h = pl.program_id(1)
    qt = pl.program_id(2)
    lam = par_ref[0]
    slope2 = par_ref[1 + h] * LOG2E
    assert tq == 2 * tk

    @pl.when(qt == 0)
    def _fill():
        ones_row = (lax.broadcasted_iota(jnp.int32, (vt.shape[1] - HEAD_DIM, tk), 0) == 0).astype(BF16)
        for j in range(k_ref.shape[1] // tk):
            vt[j, :HEAD_DIM, :] = v_ref[0, j * tk:(j + 1) * tk, :].astype(F32).T.astype(BF16)
            vt[j, HEAD_DIM:, :] = ones_row
        jl = lax.broadcasted_iota(jnp.int32, (tk, HEAD_DIM), 0).astype(F32) * slope2
        hi = jl.astype(BF16).astype(F32)
        mid = (jl - hi).astype(BF16).astype(F32)
        lo = jl - hi - mid
        lane = lax.broadcasted_iota(jnp.int32, (tk, HEAD_DIM), 1)
        kx_s[...] = jnp.where(lane == 0, hi, jnp.where(lane == 1, mid, jnp.where(lane == 2, lo, 0.0))).astype(BF16)
        jj = lax.broadcasted_iota(jnp.int32, (tk, tq), 0)
        ii = lax.broadcasted_iota(jnp.int32, (tk, tq), 1)
        mask_s[...] = jnp.where(ii >= jj, 0.0, -jnp.inf)

    q = q_ref[0].astype(F32) * (QK_DIM ** -0.5 * LOG2E)
    lane = lax.broadcasted_iota(jnp.int32, q.shape, 1)
    qx = (lane < 3).astype(F32).astype(BF16)
    qs = (jnp.concatenate([jnp.where(lane < QK_DIM, q, 0.0).astype(BF16), qx], axis=1),
          jnp.concatenate([jnp.where(lane >= QK_DIM, q, 0.0).astype(BF16), qx], axis=1))
    c0 = -slope2 * lax.broadcasted_iota(jnp.int32, (1, tq), 1).astype(F32)

    whole = slice(0, tq)

    def qk(kt, j, cols=whole):
        keys = k_ref[0, pl.ds(pl.multiple_of(kt * tk, tk), tk), :]
        return lax.dot_general(jnp.concatenate([keys, kx_s[...]], axis=1), qs[j][cols], _NT,
                               preferred_element_type=F32)

    def update(j, s, kt, cols=whole):
        c = c0[:, cols] + slope2 * (jnp.zeros((1, s.shape[1]), jnp.int32) + (kt * tk - qt * tq)).astype(F32)
        m_old = m_s[j, :, cols]
        m_new = jnp.maximum(m_old, jnp.max(s, axis=0, keepdims=True) + c)
        alpha = jnp.exp2(m_old - m_new)
        p = jnp.exp2(s - (m_new - c))
        m_s[j, :, cols] = m_new
        acc_s[j, :, cols] = alpha * acc_s[j, :, cols] + jnp.dot(vt[kt], p.astype(BF16),
                                                                preferred_element_type=F32)

    m_s[...] = jnp.full(m_s.shape, -jnp.inf, F32)
    acc_s[...] = jnp.zeros(acc_s.shape, F32)
    for j in range(2):
        sx[j] = qk(0, j)

    def body(u, carry):
        ka = 2 * u
        for j in range(2):
            sy[j] = qk(ka + 1, j)
        for j in range(2):
            update(j, sx[j], ka)
        for j in range(2):
            sx[j] = qk(ka + 2, j)
        for j in range(2):
            update(j, sy[j], ka + 1)
        return carry

    lax.fori_loop(0, qt, body, 0)

    kd = 2 * qt
    late = slice(tk, tq)
    for j in range(2):
        sy[j, :, late] = qk(kd + 1, j, late)
    for j in range(2):
        update(j, sx[j] + mask_s[...], kd)
    for j in range(2):
        update(j, sy[j, :, late] + mask_s[:, :tk], kd + 1, late)

    o0 = acc_s[0]
    o1 = acc_s[1]
    od = (o0[:HEAD_DIM] / o0[HEAD_DIM:HEAD_DIM + 1] - lam * (o1[:HEAD_DIM] / o1[HEAD_DIM:HEAD_DIM + 1])).T
    y = od * lax.rsqrt(jnp.mean(od * od, axis=-1, keepdims=True) + SUBLN_EPS) * sg_ref[...] * out_scale
    o_ref[...] = (y * _silu(g_ref[0])).astype(o_ref.dtype)


def _attn_prompt(qkv, gh, par, subln_g, batch, seq, n_heads, out_scale, tq=1024, tk=512):
    nq = seq // tq
    kern = functools.partial(_attn_prompt_kernel, tq=tq, tk=tk, out_scale=out_scale)
    return pl.pallas_call(
        kern,
        out_shape=jax.ShapeDtypeStruct((batch * seq, n_heads * HEAD_DIM), BF16),
        grid=(batch, n_heads, nq),
        in_specs=[pl.BlockSpec(memory_space=pltpu.SMEM),
                  pl.BlockSpec((1, tq, HEAD_DIM), lambda b, h, i: (0, b * nq + i, h)),
                  pl.BlockSpec((1, seq, HEAD_DIM), lambda b, h, i: (1, b, h)),
                  pl.BlockSpec((1, seq, HEAD_DIM), lambda b, h, i: (2, b, h)),
                  pl.BlockSpec((1, tq, HEAD_DIM), lambda b, h, i: (0, b * nq + i, h)),
                  pl.BlockSpec((1, HEAD_DIM), lambda b, h, i: (0, 0))],
        out_specs=pl.BlockSpec((tq, HEAD_DIM), lambda b, h, i: (b * nq + i, h)),
        scratch_shapes=[pltpu.VMEM((seq // tk, HEAD_DIM + 16, tk), BF16),
                        pltpu.VMEM((tk, HEAD_DIM), BF16),
                        pltpu.VMEM((tk, tq), F32),
                        pltpu.VMEM((2, tk, tq), F32),
                        pltpu.VMEM((2, tk, tq), F32),
                        pltpu.VMEM((2, 1, tq), F32),
                        pltpu.VMEM((2, HEAD_DIM + 16, tq), F32)],
        compiler_params=_params("arbitrary", "arbitrary", "arbitrary"),
        name="attn_prompt",
    )(par, qkv, qkv, qkv, gh, subln_g.reshape(1, HEAD_DIM))


def _attn_sample_kernel(pt_ref, par_ref, slope_ref, q_ref, kn_ref, vn_ref, g_ref, sg_ref, *rest,
                        npg, page, n_new, n_heads, out_scale):
    k_refs = rest[:npg]
    v_refs = rest[npg:2 * npg]
    o_ref = rest[2 * npg]
    qbd, kbf, vbf, b0_s, m_s, l_s, acc_s = rest[2 * npg + 1:]
    i = pl.program_id(1)
    n_steps = pl.num_programs(1)
    tblk = npg * page
    nrow, width = qbd.shape
    rows_per_head = 2 * n_new
    slope = slope_ref[...]

    def head_rows(hh):
        return slice(hh * rows_per_head, (hh + 1) * rows_per_head)

    @pl.when(i == 0)
    def _init():
        q = q_ref[0] * (QK_DIM ** -0.5)
        qt = jnp.concatenate([q] * (nrow // n_new), axis=0)
        rr = lax.broadcasted_iota(jnp.int32, (nrow, width), 0)
        cc = lax.broadcasted_iota(jnp.int32, (nrow, width), 1)
        qbd[...] = jnp.where(rr // n_new == cc // QK_DIM, qt, 0.0).astype(BF16)
        b0_s[...] = slope * lax.broadcasted_iota(jnp.int32, (nrow, tblk), 1).astype(F32)
        zpad = jnp.zeros((nrow - n_new, width), F32)
        kn = jnp.concatenate([kn_ref[0], zpad], axis=0).astype(BF16)
        vn = jnp.concatenate([vn_ref[0], zpad], axis=0).astype(BF16)
        s = lax.dot_general(qbd[...], kn, _NT, preferred_element_type=F32)
        tk = lax.broadcasted_iota(jnp.int32, (nrow, nrow), 1)
        tqry = lax.broadcasted_iota(jnp.int32, (nrow, nrow), 0) % n_new
        s = jnp.where(tk <= tqry, s + slope * tk.astype(F32), -jnp.inf)
        m = jnp.max(s, axis=1, keepdims=True)
        p = jnp.exp(s - m)
        m_s[...] = m
        l_s[...] = jnp.sum(p, axis=1, keepdims=True)
        pb = p.astype(BF16)
        for hh in range(n_heads):
            acc_s[head_rows(hh), :] = jnp.dot(pb[head_rows(hh), :], vn[:, hh * HEAD_DIM:(hh + 1) * HEAD_DIM],
                                              preferred_element_type=F32)

    for pg in range(npg):
        kbf[:, pg * page:(pg + 1) * page] = k_refs[pg][...].astype(BF16)
        for hh in range(n_heads):
            vbf[hh, pg * page:(pg + 1) * page, :] = v_refs[pg][pl.ds(hh, page, stride=n_heads), :].astype(BF16)
    s = jnp.dot(qbd[...], kbf[...], preferred_element_type=F32) + b0_s[...]
    c = slope * (jnp.zeros((nrow, 1), jnp.int32) + (i - n_steps) * tblk).astype(F32)
    m_old = m_s[...]
    m_new = jnp.maximum(m_old, jnp.max(s, axis=1, keepdims=True) + c)
    alpha = jnp.exp(m_old - m_new)
    p = jnp.exp(s - (m_new - c))
    m_s[...] = m_new
    l_s[...] = alpha * l_s[...] + jnp.sum(p, axis=1, keepdims=True)
    pb = p.astype(BF16)
    for hh in range(n_heads):
        rs = head_rows(hh)
        acc_s[rs, :] = alpha[rs, :] * acc_s[rs, :] + jnp.dot(pb[rs, :], vbf[hh], preferred_element_type=F32)

    @pl.when(i == n_steps - 1)
    def _finish():
        lam = par_ref[0]
        o = acc_s[...] / l_s[...]
        g = g_ref[0]
        for hh in range(n_heads):
            cs = slice(hh * HEAD_DIM, (hh + 1) * HEAD_DIM)
            blk = o[head_rows(hh), :]
            od = blk[:n_new] - lam * blk[n_new:]
            y = od * lax.rsqrt(jnp.mean(od * od, axis=-1, keepdims=True) + SUBLN_EPS) * sg_ref[...] * out_scale
            o_ref[0, :, cs] = y * _silu(g[:, cs])


def _attn_sample(z, cache_kt, cache_v, page_table, par, subln_g, n_heads, n_new, out_scale, npg=16):
    dec_batch, n_pages = page_table.shape
    width, page = cache_kt.shape[1], cache_kt.shape[2]
    nrow = n_heads * 2 * n_new
    assert nrow == HEAD_DIM and n_pages % npg == 0
    tblk = npg * page
    row_head = jnp.arange(nrow) // (2 * n_new)
    slope_col = jnp.exp2(-8.0 * (row_head + 1).astype(F32) / n_heads).reshape(nrow, 1)

    def page_spec(pg, rows, cols):
        return pl.BlockSpec((None, rows, cols), lambda s, i, pt: (pt[s, i * npg + pg], 0, 0))

    def sec_spec(sec):
        return pl.BlockSpec((1, None, n_new, width), lambda s, i, pt: (sec, s, 0, 0))

    kern = functools.partial(_attn_sample_kernel, npg=npg, page=page, n_new=n_new, n_heads=n_heads,
                             out_scale=out_scale)
    grid_spec = pltpu.PrefetchScalarGridSpec(
        num_scalar_prefetch=1,
        grid=(dec_batch, n_pages // npg),
        in_specs=[pl.BlockSpec(memory_space=pltpu.SMEM),
                  pl.BlockSpec((nrow, 1), lambda s, i, pt: (0, 0)),
                  sec_spec(0), sec_spec(1), sec_spec(2), sec_spec(3),
                  pl.BlockSpec((1, HEAD_DIM), lambda s, i, pt: (0, 0))]
                 + [page_spec(pg, width, page) for pg in range(npg)]
                 + [page_spec(pg, page * n_heads, HEAD_DIM) for pg in range(npg)],
        out_specs=pl.BlockSpec((1, n_new, width), lambda s, i, pt: (s, 0, 0)),
        scratch_shapes=[pltpu.VMEM((nrow, width), BF16),
                        pltpu.VMEM((width, tblk), BF16),
                        pltpu.VMEM((n_heads, tblk, HEAD_DIM), BF16),
                        pltpu.VMEM((nrow, tblk), F32),
                        pltpu.VMEM((nrow, 1), F32),
                        pltpu.VMEM((nrow, 1), F32),
                        pltpu.VMEM((nrow, HEAD_DIM), F32)],
    )
    return pl.pallas_call(
        kern,
        out_shape=jax.ShapeDtypeStruct((dec_batch, n_new, width), F32),
        grid_spec=grid_spec,
        compiler_params=_params("arbitrary", "arbitrary"),
        name="attn_sample",
    )(page_table, par, slope_col, z, z, z, z, subln_g.reshape(1, HEAD_DIM),
      *([cache_kt] * npg), *([cache_v] * npg))


def _hgrn_kernel(*refs, chunk, n_heads, has_init, chunk_is_sequence):
    if has_init:
        (q_ref, f_ref, v_ref, g_ref, lb_ref, ng_ref, s0_ref, o_ref, sout_ref,
         st, b_s, k_s, qe_s, ke_s, o_s, u_s, stc_s) = refs
    else:
        (q_ref, f_ref, v_ref, g_ref, lb_ref, ng_ref, o_ref, sout_ref,
         st, b_s, k_s, qe_s, ke_s, o_s, u_s, stc_s) = refs
        s0_ref = None
    i = pl.program_id(2)
    tb, width = f_ref.shape[1], f_ref.shape[2]
    nchunk = tb // chunk
    grp = min(tb, HG_GROUP)
    levels = [w for w in (32, 16, 8) if 2 * w <= chunk]
    assert chunk % HG_BLOCK == 0 and grp % chunk == 0 and tb % grp == 0
    heads = [slice(hh * HEAD_DIM, (hh + 1) * HEAD_DIM) for hh in range(n_heads)]

    if not chunk_is_sequence:
        @pl.when(i == 0)
        def _init():
            for hh in range(n_heads):
                st[hh] = s0_ref[0, hh].T if has_init else jnp.zeros((HEAD_DIM, HEAD_DIM), F32)

    lb = lb_ref[...]
    forget = lb + (1.0 - lb) * jax.nn.sigmoid(f_ref[0])
    kk = 1.0 - forget
    b = jnp.log(forget) * LOG2E
    row = lax.broadcasted_iota(jnp.int32, b.shape, 0) % chunk
    sh = 1
    while sh < chunk:
        b = b + jnp.where(row >= sh, pltpu.roll(b, sh, axis=0), 0.0)
        sh *= 2
    b3 = b.reshape(nchunk, chunk, width)
    b_end = jnp.broadcast_to(b3[:, chunk - 1:chunk, :], b3.shape).reshape(tb, width)
    b_s[...] = b
    k_s[...] = kk
    qe_s[...] = q_ref[0] * jnp.exp2(b)
    ke_s[...] = kk * jnp.exp2(b_end - b)

    rowi = lax.broadcasted_iota(jnp.int32, (grp, HEAD_DIM), 0)
    ti = lax.broadcasted_iota(jnp.int32, (grp, grp), 0)
    si = lax.broadcasted_iota(jnp.int32, (grp, grp), 1)

    def block_row(x, period, r):
        x3 = x.reshape(grp // period, period, HEAD_DIM)
        return jnp.broadcast_to(x3[:, r:r + 1, :], x3.shape).reshape(grp, HEAD_DIM)

    def neg_unless(cond):
        return jnp.where(cond, 0.0, -jnp.inf)

    same_block = ti // HG_BLOCK == si // HG_BLOCK
    row_from = [neg_unless(rowi % HG_BLOCK >= s) for s in range(HG_BLOCK)]
    pair_at = [(same_block & (si % HG_BLOCK == s)).astype(F32) for s in range(HG_BLOCK)]
    split = [(neg_unless(rowi % (2 * w) >= w), neg_unless(rowi % (2 * w) < w),
              (ti // (2 * w) == si // (2 * w)).astype(F32)) for w in levels]

    def group_body(gi, carry):
        rs = pl.ds(pl.multiple_of(gi * grp, grp), grp)
        for cs in heads:
            qg = q_ref[0, rs, cs]
            kg = k_s[rs, cs]
            bg = b_s[rs, cs]
            kgb = kg.astype(BF16)
            a = jnp.zeros((grp, grp), F32)
            for s in range(HG_BLOCK):
                e = jnp.exp2(bg - block_row(bg, HG_BLOCK, s) + row_from[s])
                x = lax.dot_general((qg * e).astype(BF16), kgb, _NT, preferred_element_type=F32)
                a = a + x * pair_at[s]
            for w, (later, earlier, same_parent) in zip(levels, split):
                d = bg - block_row(bg, 2 * w, w)
                x = lax.dot_general((qg * jnp.exp2(d + later)).astype(BF16),
                                    (kg * jnp.exp2(earlier - d)).astype(BF16), _NT,
                                    preferred_element_type=F32)
                a = a + x * same_parent
            o_s[rs, cs] = jnp.dot(a.astype(BF16), v_ref[0, rs, cs].astype(BF16), preferred_element_type=F32)
        return carry

    lax.fori_loop(0, tb // grp, group_body, 0, unroll=math.gcd(tb // grp, 4))

    for hh, cs in enumerate(heads):
        for c in range(nchunk):
            rs = slice(c * chunk, (c + 1) * chunk)
            u_s[c] = lax.dot_general(v_ref[0, rs, cs].astype(BF16), ke_s[rs, cs].astype(BF16), _TN,
                                     preferred_element_type=F32)
        s_cur = None if chunk_is_sequence else st[hh]
        for c in range(nchunk):
            if chunk_is_sequence:
                s_cur = s0_ref[c, hh].T
            stc_s[c] = s_cur.astype(BF16)
            last = (c + 1) * chunk - 1
            s_cur = s_cur * jnp.exp2(b_s[last:last + 1, cs]) + u_s[c]
            if chunk_is_sequence:
                sout_ref[c, hh] = s_cur.T
        if not chunk_is_sequence:
            st[hh] = s_cur
        for c in range(nchunk):
            rs = slice(c * chunk, (c + 1) * chunk)
            o_s[rs, cs] += lax.dot_general(qe_s[rs, cs].astype(BF16), stc_s[c], _NT,
                                           preferred_element_type=F32)

    for cs in heads:
        o = o_s[:, cs]
        y = o * lax.rsqrt(jnp.mean(o * o, axis=-1, keepdims=True) + NORM_EPS) * ng_ref[...]
        o_ref[0, :, cs] = (y * _silu(g_ref[0, :, cs])).astype(o_ref.dtype)

    if not chunk_is_sequence:
        @pl.when(i == pl.num_programs(2) - 1)
        def _emit():
            for hh in range(n_heads):
                sout_ref[0, hh] = st[hh].T


def _hgrn(z, sec0, lb, norm_g, s0, batch, length, n_heads, heads_per_step, tb, chunk, out_dtype, seqs_per_step=1):
    width = z.shape[2]
    wblk = heads_per_step * HEAD_DIM
    has_init = s0 is not None
    chunk_is_sequence = seqs_per_step > 1
    if chunk_is_sequence:
        assert has_init and tb == chunk == length and batch % seqs_per_step == 0
        batch, length, tb = batch // seqs_per_step, length * seqs_per_step, tb * seqs_per_step
    nblk = length // tb

    def sec_spec(sec):
        return pl.BlockSpec((1, tb, wblk), lambda b, h, i: (sec, b * nblk + i, h))

    in_specs = [sec_spec(sec0), sec_spec(sec0 + 1), sec_spec(sec0 + 2), sec_spec(sec0 + 3),
                pl.BlockSpec((1, wblk), lambda b, h, i: (0, h)),
                pl.BlockSpec((1, HEAD_DIM), lambda b, h, i: (0, 0))]
    args = [z, z, z, z, lb.reshape(1, width), norm_g.reshape(1, HEAD_DIM)]
    state_spec = pl.BlockSpec((seqs_per_step, heads_per_step, HEAD_DIM, HEAD_DIM), lambda b, h, i: (b, h, 0, 0))
    if has_init:
        in_specs.append(state_spec)
        args.append(s0)
    kern = functools.partial(_hgrn_kernel, chunk=chunk, n_heads=heads_per_step, has_init=has_init,
                             chunk_is_sequence=chunk_is_sequence)
    return pl.pallas_call(
        kern,
        out_shape=(jax.ShapeDtypeStruct((1, batch * length, width), out_dtype),
                   jax.ShapeDtypeStruct((batch * seqs_per_step, n_heads, HEAD_DIM, HEAD_DIM), F32)),
        grid=(batch, n_heads // heads_per_step, nblk),
        in_specs=in_specs,
        out_specs=(pl.BlockSpec((1, tb, wblk), lambda b, h, i: (0, b * nblk + i, h)), state_spec),
        scratch_shapes=[pltpu.VMEM((heads_per_step, HEAD_DIM, HEAD_DIM), F32)]
                       + [pltpu.VMEM((tb, wblk), F32)] * 5
                       + [pltpu.VMEM((tb // chunk, HEAD_DIM, HEAD_DIM), F32),
                          pltpu.VMEM((tb // chunk, HEAD_DIM, HEAD_DIM), BF16)],
        compiler_params=_params("arbitrary", "arbitrary", "arbitrary"),
        name="hgrn",
    )(*args)


def _outproj_kernel(ma_ref, mh_ref, w_ref, x_ref, rg_ref, fg_ref, y_ref, *, final_norm):
    half = ma_ref.shape[1]
    u = jnp.dot(ma_ref[...].astype(BF16), w_ref[:half, :], preferred_element_type=F32)
    u = u + jnp.dot(mh_ref[...].astype(BF16), w_ref[half:, :], preferred_element_type=F32)
    r = x_ref[...] + rg_ref[0] * u
    if final_norm:
        r = r * lax.rsqrt(jnp.mean(r * r, axis=-1, keepdims=True) + NORM_EPS) * fg_ref[...]
    y_ref[...] = r


def _outproj(m_att, m_hg, w_bf, x, rg, final_g, tm, rows_per_mod, final_norm):
    r, d = x.shape
    half = m_att.shape[1]
    rb = rg.shape[1]
    tiles_per_mod = rows_per_mod // tm
    return pl.pallas_call(
        functools.partial(_outproj_kernel, final_norm=final_norm),
        out_shape=jax.ShapeDtypeStruct((r, d), F32),
        grid=(r // tm,),
        in_specs=[pl.BlockSpec((tm, half), lambda i: (i, 0)),
                  pl.BlockSpec((tm, half), lambda i: (i, 0)),
                  pl.BlockSpec((2 * half, d), lambda i: (0, 0)),
                  pl.BlockSpec((tm, d), lambda i: (i, 0)),
                  pl.BlockSpec((1, rb, d), lambda i: (i // tiles_per_mod, 0, 0)),
                  pl.BlockSpec((1, d), lambda i: (0, 0))],
        out_specs=pl.BlockSpec((tm, d), lambda i: (i, 0)),
        compiler_params=_params("arbitrary"),
        name="outproj",
    )(m_att, m_hg, w_bf, x, rg, final_g.reshape(1, d))


def kernel(x_prompt, x_sample, cache_k, cache_v, state_hgrn, page_table, c_prompt, c_sample,
           norm_g, w_ada, b_ada, w_in, lambda_q1, lambda_k1, lambda_q2, lambda_k2, subln_g,
           hg_lower_bounds, hg_norm_g, w_out, final_g):
    batch, seq, d = x_prompt.shape
    dec_batch, n_new, _ = x_sample.shape
    depth, n_pool, page, n_heads, _, qk_dim = cache_k.shape
    hg_heads = state_hgrn.shape[2]
    width = d // 2
    assert qk_dim == QK_DIM and n_heads * HEAD_DIM == width and hg_heads * HEAD_DIM == width
    assert w_in.shape[2] == N_SECTIONS * width

    slopes = jnp.asarray([2.0 ** (-8.0 * (h + 1) / n_heads) for h in range(n_heads)], F32)
    lb_all = jnp.cumsum(jax.nn.softmax(hg_lower_bounds.astype(F32), axis=0), axis=0)
    n_mod = batch + dec_batch
    n_mod_pad = -(-n_mod // 8) * 8
    c_all = jnp.concatenate([c_prompt, c_sample, jnp.zeros((n_mod_pad - n_mod, d), F32)], axis=0)

    xp = x_prompt.reshape(batch * seq, d)
    xs = x_sample.reshape(dec_batch * n_new, d)
    rs = dec_batch * n_new
    outs = [[] for _ in range(6)]
    for l in range(depth):
        last = l == depth - 1
        lam_init = 0.8 - 0.6 * math.exp(-0.3 * l)
        lam = (jnp.exp(jnp.sum(lambda_q1[l].astype(F32) * lambda_k1[l].astype(F32)))
               - jnp.exp(jnp.sum(lambda_q2[l].astype(F32) * lambda_k2[l].astype(F32))) + lam_init)
        par = jnp.concatenate([lam.reshape(1), slopes, jnp.zeros((7,), F32)])
        w_in_bf = w_in[l].astype(BF16)
        w_out_bf = w_out[l].astype(BF16)

        mod = _adaln(c_all, w_ada[l], b_ada[l])
        shift, scale, rgate = mod[:, :d], mod[:, d:2 * d], mod[:, 2 * d:]

        def rows_p(t):
            return t[:batch].reshape(batch, 1, d)

        def rows_s(t):
            return jnp.repeat(t[batch:n_mod], n_new, axis=0).reshape(1, rs, d)

        hp = _modnorm(xp, norm_g[l], rows_p(scale), rows_p(shift), tm=512, rows_per_mod=seq)
        qkv, gh, k_t, v_rows = _inproj_prompt(hp, w_in_bf, batch, seq, n_heads, tm=min(seq, 1024))
        att_p = _attn_prompt(qkv, gh, par, subln_g[l], batch, seq, n_heads, 1.0 - lam_init)
        hg_p, s_p = _hgrn(gh, 1, lb_all[l], hg_norm_g[l], None, batch, seq, hg_heads, heads_per_step=1,
                          tb=min(seq, 1024), chunk=math.gcd(seq, HG_CHUNK), out_dtype=BF16)
        xp = _outproj(att_p, hg_p[0], w_out_bf, xp, rows_p(rgate), final_g,
                      tm=512, rows_per_mod=seq, final_norm=last)
        k_p = jnp.transpose(k_t.reshape(batch, n_heads, 2, QK_DIM, seq), (0, 4, 1, 2, 3))
        outs[0].append(k_p)
        outs[1].append(v_rows.reshape(batch, seq, n_heads, HEAD_DIM))
        outs[2].append(s_p)

        hs = _modnorm(xs, norm_g[l], rows_s(scale), rows_s(shift), tm=rs, rows_per_mod=rs)
        zs = _inproj(hs, w_in_bf, tm=rs)
        cache_kt = jnp.transpose(cache_k[l], (0, 2, 3, 4, 1)).reshape(n_pool, width, page)
        cache_vr = cache_v[l].reshape(n_pool, page * n_heads, HEAD_DIM)
        att_s = _attn_sample(zs.reshape(N_SECTIONS, dec_batch, n_new, width), cache_kt, cache_vr,
                             page_table, par, subln_g[l], n_heads, n_new, 1.0 - lam_init)
        hg_s, s_s = _hgrn(zs, 4, lb_all[l], hg_norm_g[l], state_hgrn[l], dec_batch, n_new, hg_heads,
                          heads_per_step=hg_heads, tb=n_new, chunk=n_new, out_dtype=F32,
                          seqs_per_step=math.gcd(dec_batch, 8))
        xs = _outproj(att_s.reshape(rs, width), hg_s[0], w_out_bf, xs, rows_s(rgate), final_g,
                      tm=rs, rows_per_mod=rs, final_norm=last)
        outs[3].append(zs[1].reshape(dec_batch, n_new, n_heads, 2, QK_DIM))
        outs[4].append(zs[2].reshape(dec_batch, n_new, n_heads, HEAD_DIM))
        outs[5].append(s_s)

    y_prompt = xp.reshape(batch, seq, d)
    y_sample = xs.reshape(dec_batch, n_new, d)
    return (y_prompt, y_sample) + tuple(jnp.stack(o) for o in outs)
```

```python
import functools
import math

import jax
import jax.numpy as jnp
from jax import lax
from jax.experimental import pallas as pl
from jax.experimental.pallas import tpu as pltpu

F32 = jnp.float32
BF16 = jnp.bfloat16

NORM_EPS = 1e-6
SUBLN_EPS = 1e-5
N_SECTIONS = 8
HEAD_DIM = 128
QK_DIM = 64
HG_CHUNK = 64
HG_BLOCK = 8
HG_GROUP = 128
LOG2E = 1.4426950408889634
SAMPLE_PAGES_PER_STEP = 16
VMEM_LIMIT_BYTES = 56 * 1024 * 1024

_NT = (((1,), (1,)), ((), ()))
_TN = (((0,), (0,)), ((), ()))


def _params(*sem):
    return pltpu.CompilerParams(dimension_semantics=sem, vmem_limit_bytes=VMEM_LIMIT_BYTES)


def _silu(x):
    return x * jax.nn.sigmoid(x)


def _adaln_kernel(c_ref, w_ref, b_ref, o_ref):
    a = _silu(c_ref[...]).astype(BF16)
    o_ref[...] = jnp.dot(a, w_ref[...].astype(BF16), preferred_element_type=F32) + b_ref[...]


def _adaln(c, w, b, tn=1024):
    m, d = c.shape
    n = w.shape[1]
    return pl.pallas_call(
        _adaln_kernel,
        out_shape=jax.ShapeDtypeStruct((m, n), F32),
        grid=(n // tn,),
        in_specs=[pl.BlockSpec((m, d), lambda j: (0, 0)),
                  pl.BlockSpec((d, tn), lambda j: (0, j)),
                  pl.BlockSpec((1, tn), lambda j: (0, j))],
        out_specs=pl.BlockSpec((m, tn), lambda j: (0, j)),
        compiler_params=_params("arbitrary"),
        name="adaln",
    )(c, w, b.reshape(1, n))


def _modnorm_kernel(x_ref, g_ref, sc_ref, sh_ref, h_ref):
    x = x_ref[...]
    y = x * lax.rsqrt(jnp.mean(x * x, axis=-1, keepdims=True) + NORM_EPS) * g_ref[...]
    h_ref[...] = (y * (1.0 + sc_ref[0]) + sh_ref[0]).astype(h_ref.dtype)


def _modnorm(x, g, scale, shift, tm, rows_per_mod):
    r, d = x.shape
    rb = scale.shape[1]
    tiles_per_mod = rows_per_mod // tm
    mod_spec = pl.BlockSpec((1, rb, d), lambda i: (i // tiles_per_mod, 0, 0))
    return pl.pallas_call(
        _modnorm_kernel,
        out_shape=jax.ShapeDtypeStruct((r, d), BF16),
        grid=(r // tm,),
        in_specs=[pl.BlockSpec((tm, d), lambda i: (i, 0)),
                  pl.BlockSpec((1, d), lambda i: (0, 0)),
                  mod_spec, mod_spec],
        out_specs=pl.BlockSpec((tm, d), lambda i: (i, 0)),
        compiler_params=_params("arbitrary"),
        name="modnorm",
    )(x, g.reshape(1, d), scale, shift)


def _inproj_kernel(h_ref, w_ref, o_ref):
    o_ref[0] = jnp.dot(h_ref[...], w_ref[...], preferred_element_type=F32)


def _inproj(h, w, tm):
    r, d = h.shape
    wsec = w.shape[1] // N_SECTIONS
    return pl.pallas_call(
        _inproj_kernel,
        out_shape=jax.ShapeDtypeStruct((N_SECTIONS, r, wsec), F32),
        grid=(r // tm, N_SECTIONS),
        in_specs=[pl.BlockSpec((tm, d), lambda i, s: (i, 0)),
                  pl.BlockSpec((d, wsec), lambda i, s: (0, s))],
        out_specs=pl.BlockSpec((1, tm, wsec), lambda i, s: (s, i, 0)),
        compiler_params=_params("arbitrary", "arbitrary"),
        name="inproj",
    )(h, w)


def _inproj_prompt_kernel(h_ref, w_ref, qkv_ref, gh_ref, kt_ref, vo_ref, *, n_heads):
    s = pl.program_id(1)
    tm = h_ref.shape[0]
    z = jnp.dot(h_ref[...], w_ref[...], preferred_element_type=F32)

    @pl.when(s <= 2)
    def _():
        qkv_ref[0] = z.astype(BF16)

    @pl.when(s == 1)
    def _():
        kt_ref[0] = z.T

    @pl.when(s == 2)
    def _():
        for hh in range(n_heads):
            vo_ref[pl.ds(hh, tm, stride=n_heads), :] = z[:, hh * HEAD_DIM:(hh + 1) * HEAD_DIM]

    @pl.when(s >= 3)
    def _():
        gh_ref[0] = z


def _inproj_prompt(h, w, batch, seq, n_heads, tm):
    r, d = h.shape
    wsec = w.shape[1] // N_SECTIONS
    tpb = seq // tm
    kern = functools.partial(_inproj_prompt_kernel, n_heads=n_heads)
    return pl.pallas_call(
        kern,
        out_shape=(jax.ShapeDtypeStruct((3, r, wsec), BF16),
                   jax.ShapeDtypeStruct((N_SECTIONS - 3, r, wsec), F32),
                   jax.ShapeDtypeStruct((batch, wsec, seq), F32),
                   jax.ShapeDtypeStruct((r * n_heads, HEAD_DIM), F32)),
        grid=(r // tm, N_SECTIONS),
        in_specs=[pl.BlockSpec((tm, d), lambda i, s: (i, 0)),
                  pl.BlockSpec((d, wsec), lambda i, s: (0, s))],
        out_specs=(pl.BlockSpec((1, tm, wsec), lambda i, s: (jnp.minimum(s, 2), i, 0)),
                   pl.BlockSpec((1, tm, wsec), lambda i, s: (jnp.maximum(s, 3) - 3, i, 0)),
                   pl.BlockSpec((1, wsec, tm), lambda i, s: (i // tpb, 0, i % tpb)),
                   pl.BlockSpec((tm * n_heads, HEAD_DIM), lambda i, s: (i, 0))),
        compiler_params=_params("arbitrary", "arbitrary"),
        name="inproj_prompt",
    )(h, w)


def _prompt_attn_ops(par_ref, q_ref, k_ref, v_ref, g_ref, sg_ref, o_ref,
                     vt, kx_s, mask_s, sx, sy, m_s, acc_s, *, h, qt, tq, tk, out_scale):
    lam = par_ref[0]
    slope2 = par_ref[1 + h] * LOG2E
    assert tq == 2 * tk

    def fill():
        ones_row = (lax.broadcasted_iota(jnp.int32, (vt.shape[1] - HEAD_DIM, tk), 0) == 0).astype(BF16)
        for j in range(k_ref.shape[1] // tk):
            vt[j, :HEAD_DIM, :] = v_ref[0, j * tk:(j + 1) * tk, :].astype(F32).T.astype(BF16)
            vt[j, HEAD_DIM:, :] = ones_row
        jl = lax.broadcasted_iota(jnp.int32, (tk, HEAD_DIM), 0).astype(F32) * slope2
        hi = jl.astype(BF16).astype(F32)
        mid = (jl - hi).astype(BF16).astype(F32)
        lo = jl - hi - mid
        lane = lax.broadcasted_iota(jnp.int32, (tk, HEAD_DIM), 1)
        kx_s[...] = jnp.where(lane == 0, hi, jnp.where(lane == 1, mid, jnp.where(lane == 2, lo, 0.0))).astype(BF16)
        jj = lax.broadcasted_iota(jnp.int32, (tk, tq), 0)
        ii = lax.broadcasted_iota(jnp.int32, (tk, tq), 1)
        mask_s[...] = jnp.where(ii >= jj, 0.0, -jnp.inf)

    q = q_ref[0].astype(F32) * (QK_DIM ** -0.5 * LOG2E)
    lane = lax.broadcasted_iota(jnp.int32, q.shape, 1)
    qx = (lane < 3).astype(F32).astype(BF16)
    qs = (jnp.concatenate([jnp.where(lane < QK_DIM, q, 0.0).astype(BF16), qx], axis=1),
          jnp.concatenate([jnp.where(lane >= QK_DIM, q, 0.0).astype(BF16), qx], axis=1))
    c0 = -slope2 * lax.broadcasted_iota(jnp.int32, (1, tq), 1).astype(F32)

    whole = slice(0, tq)

    def qk(kt, j, cols=whole):
        keys = k_ref[0, pl.ds(pl.multiple_of(kt * tk, tk), tk), :]
        return lax.dot_general(jnp.concatenate([keys, kx_s[...]], axis=1), qs[j][cols], _NT,
                               preferred_element_type=F32)

    def update(j, s, kt, cols=whole):
        c = c0[:, cols] + slope2 * (jnp.zeros((1, s.shape[1]), jnp.int32) + (kt * tk - qt * tq)).astype(F32)
        m_old = m_s[j, :, cols]
        m_new = jnp.maximum(m_old, jnp.max(s, axis=0, keepdims=True) + c)
        alpha = jnp.exp2(m_old - m_new)
        p = jnp.exp2(s - (m_new - c))
        m_s[j, :, cols] = m_new
        acc_s[j, :, cols] = alpha * acc_s[j, :, cols] + jnp.dot(vt[kt], p.astype(BF16),
                                                                preferred_element_type=F32)

    def start():
        m_s[...] = jnp.full(m_s.shape, -jnp.inf, F32)
        acc_s[...] = jnp.zeros(acc_s.shape, F32)
        for j in range(2):
            sx[j] = qk(0, j)

    def step(u):
        ka = 2 * u
        for j in range(2):
            sy[j] = qk(ka + 1, j)
        for j in range(2):
            update(j, sx[j], ka)
        for j in range(2):
            sx[j] = qk(ka + 2, j)
        for j in range(2):
            update(j, sy[j], ka + 1)

    def finish():
        kd = 2 * qt
        late = slice(tk, tq)
        for j in range(2):
            sy[j, :, late] = qk(kd + 1, j, late)
        for j in range(2):
            update(j, sx[j] + mask_s[...], kd)
        for j in range(2):
            update(j, sy[j, :, late] + mask_s[:, :tk], kd + 1, late)
        o0 = acc_s[0]
        o1 = acc_s[1]
        od = (o0[:HEAD_DIM] / o0[HEAD_DIM:HEAD_DIM + 1] - lam * (o1[:HEAD_DIM] / o1[HEAD_DIM:HEAD_DIM + 1])).T
        y = od * lax.rsqrt(jnp.mean(od * od, axis=-1, keepdims=True) + SUBLN_EPS) * sg_ref[...] * out_scale
        o_ref[...] = (y * _silu(g_ref[0])).astype(o_ref.dtype)

    return fill, start, step, finish


def _prompt_attn_scratch(seq, tq, tk):
    return [pltpu.VMEM((seq // tk, HEAD_DIM + 16, tk), BF16),
            pltpu.VMEM((tk, HEAD_DIM), BF16),
            pltpu.VMEM((tk, tq), F32),
            pltpu.VMEM((2, tk, tq), F32),
            pltpu.VMEM((2, tk, tq), F32),
            pltpu.VMEM((2, 1, tq), F32),
            pltpu.VMEM((2, HEAD_DIM + 16, tq), F32)]


def _attn_prompt_kernel(*refs, tq, tk, out_scale):
    qt = pl.program_id(2)
    fill, start, step, finish = _prompt_attn_ops(*refs, h=pl.program_id(1), qt=qt, tq=tq, tk=tk,
                                                 out_scale=out_scale)
    pl.when(qt == 0)(fill)
    start()
    lax.fori_loop(0, qt, lambda u, c: (step(u), c)[1], 0)
    finish()


def _attn_prompt(qkv, gh, par, subln_g, batch, seq, n_heads, out_scale, tq=1024, tk=512):
    nq = seq // tq
    kern = functools.partial(_attn_prompt_kernel, tq=tq, tk=tk, out_scale=out_scale)
    return pl.pallas_call(
        kern,
        out_shape=jax.ShapeDtypeStruct((batch * seq, n_heads * HEAD_DIM), BF16),
        grid=(batch, n_heads, nq),
        in_specs=[pl.BlockSpec(memory_space=pltpu.SMEM),
                  pl.BlockSpec((1, tq, HEAD_DIM), lambda b, h, i: (0, b * nq + i, h)),
                  pl.BlockSpec((1, seq, HEAD_DIM), lambda b, h, i: (1, b, h)),
                  pl.BlockSpec((1, seq, HEAD_DIM), lambda b, h, i: (2, b, h)),
                  pl.BlockSpec((1, tq, HEAD_DIM), lambda b, h, i: (0, b * nq + i, h)),
                  pl.BlockSpec((1, HEAD_DIM), lambda b, h, i: (0, 0))],
        out_specs=pl.BlockSpec((tq, HEAD_DIM), lambda b, h, i: (b * nq + i, h)),
        scratch_shapes=_prompt_attn_scratch(seq, tq, tk),
        compiler_params=_params("arbitrary", "arbitrary", "arbitrary"),
        name="attn_prompt",
    )(par, qkv, qkv, qkv, gh, subln_g.reshape(1, HEAD_DIM))


def _attn_sample_kernel(*refs, npg, page, n_new, n_heads, out_scale, prompt_tiles):
    if prompt_tiles:
        pt_ref, sched_ref, par_ref, slope_ref, q_ref, kn_ref, vn_ref, g_ref, sg_ref = refs[:9]
        p_in = refs[9:13]
        rest = refs[13:]
        o_ref, po_ref = rest[2 * npg:2 * npg + 2]
        scratch = rest[2 * npg + 2:]
    else:
        pt_ref, par_ref, slope_ref, q_ref, kn_ref, vn_ref, g_ref, sg_ref = refs[:8]
        rest = refs[8:]
        o_ref = rest[2 * npg]
        scratch = rest[2 * npg + 1:]
    k_refs = rest[:npg]
    v_refs = rest[npg:2 * npg]
    qbd, kbf, vbf, b0_s, m_s, l_s, acc_s = scratch[:7]
    i = pl.program_id(1)
    n_steps = pl.num_programs(1)
    tblk = npg * page
    nrow, width = qbd.shape
    rows_per_head = 2 * n_new
    slope = slope_ref[...]

    def head_rows(hh):
        return slice(hh * rows_per_head, (hh + 1) * rows_per_head)

    @pl.when(i == 0)
    def _init():
        q = q_ref[0] * (QK_DIM ** -0.5)
        qt = jnp.concatenate([q] * (nrow // n_new), axis=0)
        rr = lax.broadcasted_iota(jnp.int32, (nrow, width), 0)
        cc = lax.broadcasted_iota(jnp.int32, (nrow, width), 1)
        qbd[...] = jnp.where(rr // n_new == cc // QK_DIM, qt, 0.0).astype(BF16)
        b0_s[...] = slope * lax.broadcasted_iota(jnp.int32, (nrow, tblk), 1).astype(F32)
        zpad = jnp.zeros((nrow - n_new, width), F32)
        kn = jnp.concatenate([kn_ref[0], zpad], axis=0).astype(BF16)
        vn = jnp.concatenate([vn_ref[0], zpad], axis=0).astype(BF16)
        s = lax.dot_general(qbd[...], kn, _NT, preferred_element_type=F32)
        tk = lax.broadcasted_iota(jnp.int32, (nrow, nrow), 1)
        tqry = lax.broadcasted_iota(jnp.int32, (nrow, nrow), 0) % n_new
        s = jnp.where(tk <= tqry, s + slope * tk.astype(F32), -jnp.inf)
        m = jnp.max(s, axis=1, keepdims=True)
        p = jnp.exp(s - m)
        m_s[...] = m
        l_s[...] = jnp.sum(p, axis=1, keepdims=True)
        pb = p.astype(BF16)
        for hh in range(n_heads):
            acc_s[head_rows(hh), :] = jnp.dot(pb[head_rows(hh), :], vn[:, hh * HEAD_DIM:(hh + 1) * HEAD_DIM],
                                              preferred_element_type=F32)

    for pg in range(npg):
        kbf[:, pg * page:(pg + 1) * page] = k_refs[pg][...].astype(BF16)
        for hh in range(n_heads):
            vbf[hh, pg * page:(pg + 1) * page, :] = v_refs[pg][pl.ds(hh, page, stride=n_heads), :].astype(BF16)
    s = jnp.dot(qbd[...], kbf[...], preferred_element_type=F32) + b0_s[...]
    c = slope * (jnp.zeros((nrow, 1), jnp.int32) + (i - n_steps) * tblk).astype(F32)
    m_old = m_s[...]
    m_new = jnp.maximum(m_old, jnp.max(s, axis=1, keepdims=True) + c)
    alpha = jnp.exp(m_old - m_new)
    p = jnp.exp(s - (m_new - c))
    m_s[...] = m_new
    l_s[...] = alpha * l_s[...] + jnp.sum(p, axis=1, keepdims=True)
    pb = p.astype(BF16)
    for hh in range(n_heads):
        rs = head_rows(hh)
        acc_s[rs, :] = alpha[rs, :] * acc_s[rs, :] + jnp.dot(pb[rs, :], vbf[hh], preferred_element_type=F32)

    @pl.when(i == n_steps - 1)
    def _finish():
        lam = par_ref[0]
        o = acc_s[...] / l_s[...]
        g = g_ref[0]
        for hh in range(n_heads):
            cs = slice(hh * HEAD_DIM, (hh + 1) * HEAD_DIM)
            blk = o[head_rows(hh), :]
            od = blk[:n_new] - lam * blk[n_new:]
            y = od * lax.rsqrt(jnp.mean(od * od, axis=-1, keepdims=True) + SUBLN_EPS) * sg_ref[...] * out_scale
            o_ref[0, :, cs] = y * _silu(g[:, cs])

    if prompt_tiles:
        tq, tk = prompt_tiles
        row = (pl.program_id(0) * n_steps + i) * SCHED_COLS
        bh, qt, u0, n_items = (sched_ref[row + c] for c in range(SCHED_COLS))
        fill, start, step, finish = _prompt_attn_ops(
            par_ref, *p_in, sg_ref, po_ref, *scratch[7:], h=bh % n_heads, qt=qt, tq=tq, tk=tk,
            out_scale=out_scale)
        pl.when((n_items > 0) & (qt == 0) & (u0 == 0))(fill)

        def item(t, carry):
            w = u0 + t
            pl.when(w == 0)(start)
            pl.when(w < qt)(lambda: step(w))
            pl.when(w == qt)(finish)
            return carry

        lax.fori_loop(0, n_items, item, 0)


SCHED_COLS = 4
MAX_ITEMS_PER_STEP = 3


def _attn_schedule(n_bh, nq, n_steps):
    rows = []
    for bh in range(n_bh):
        for qt in range(nq):
            u0 = 0
            while u0 <= qt:
                n = min(MAX_ITEMS_PER_STEP, qt + 1 - u0)
                rows.append((bh, qt, u0, n))
                u0 += n
    if len(rows) > n_steps:
        return None
    rows += [(n_bh - 1, nq - 1, 0, 0)] * (n_steps - len(rows))
    return [v for r in rows for v in r]


def _attn_sample(z, cache_kt, cache_v, page_table, par, subln_g, n_heads, n_new, out_scale, npg=16,
                 prompt=None):
    dec_batch, n_pages = page_table.shape
    width, page = cache_kt.shape[1], cache_kt.shape[2]
    nrow = n_heads * 2 * n_new
    assert nrow == HEAD_DIM and n_pages % npg == 0
    tblk = npg * page
    n_steps = n_pages // npg
    row_head = jnp.arange(nrow) // (2 * n_new)
    slope_col = jnp.exp2(-8.0 * (row_head + 1).astype(F32) / n_heads).reshape(nrow, 1)

    def page_spec(pg, rows, cols):
        return pl.BlockSpec((None, rows, cols), lambda s, i, *_: (_[0][s, i * npg + pg], 0, 0))

    def sec_spec(sec):
        return pl.BlockSpec((1, None, n_new, width), lambda s, i, *_: (sec, s, 0, 0))

    in_specs = [pl.BlockSpec(memory_space=pltpu.SMEM),
                pl.BlockSpec((nrow, 1), lambda s, i, *_: (0, 0)),
                sec_spec(0), sec_spec(1), sec_spec(2), sec_spec(3),
                pl.BlockSpec((1, HEAD_DIM), lambda s, i, *_: (0, 0))]
    args = [par, slope_col, z, z, z, z, subln_g.reshape(1, HEAD_DIM)]
    out_specs = [pl.BlockSpec((1, n_new, width), lambda s, i, *_: (s, 0, 0))]
    out_shape = [jax.ShapeDtypeStruct((dec_batch, n_new, width), F32)]
    scratch = [pltpu.VMEM((nrow, width), BF16),
               pltpu.VMEM((width, tblk), BF16),
               pltpu.VMEM((n_heads, tblk, HEAD_DIM), BF16),
               pltpu.VMEM((nrow, tblk), F32),
               pltpu.VMEM((nrow, 1), F32),
               pltpu.VMEM((nrow, 1), F32),
               pltpu.VMEM((nrow, HEAD_DIM), F32)]
    prefetch = [page_table]
    if prompt is not None:
        qkv, gh, batch, seq, tq, tk = prompt
        nq = seq // tq
        sched = _attn_schedule(batch * n_heads, nq, dec_batch * n_steps)
        assert sched is not None
        prefetch.append(jnp.asarray(sched, jnp.int32))

        def blk(s, i, sc):
            row = (s * n_steps + i) * SCHED_COLS
            return sc[row] // n_heads, sc[row] % n_heads, sc[row + 1]

        def q_map(sec):
            def index(s, i, pt, sc):
                b, h, qt = blk(s, i, sc)
                return sec, b * nq + qt, h
            return index

        def kv_map(sec):
            def index(s, i, pt, sc):
                b, h, _ = blk(s, i, sc)
                return sec, b, h
            return index

        def out_map(s, i, pt, sc):
            b, h, qt = blk(s, i, sc)
            return b * nq + qt, h

        in_specs += [pl.BlockSpec((1, tq, HEAD_DIM), q_map(0)),
                     pl.BlockSpec((1, seq, HEAD_DIM), kv_map(1)),
                     pl.BlockSpec((1, seq, HEAD_DIM), kv_map(2)),
                     pl.BlockSpec((1, tq, HEAD_DIM), q_map(0))]
        args += [qkv, qkv, qkv, gh]
        out_specs.append(pl.BlockSpec((tq, HEAD_DIM), out_map))
        out_shape.append(jax.ShapeDtypeStruct((batch * seq, n_heads * HEAD_DIM), BF16))
        scratch += _prompt_attn_scratch(seq, tq, tk)
    in_specs += ([page_spec(pg, width, page) for pg in range(npg)]
                 + [page_spec(pg, page * n_heads, HEAD_DIM) for pg in range(npg)])
    args += [cache_kt] * npg + [cache_v] * npg

    kern = functools.partial(_attn_sample_kernel, npg=npg, page=page, n_new=n_new, n_heads=n_heads,
                             out_scale=out_scale, prompt_tiles=None if prompt is None else (tq, tk))
    grid_spec = pltpu.PrefetchScalarGridSpec(
        num_scalar_prefetch=len(prefetch),
        grid=(dec_batch, n_steps),
        in_specs=in_specs,
        out_specs=out_specs,
        scratch_shapes=scratch,
    )
    outs = pl.pallas_call(
        kern,
        out_shape=out_shape,
        grid_spec=grid_spec,
        compiler_params=_params("arbitrary", "arbitrary"),
        name="attn_sample",
    )(*prefetch, *args)
    return outs[0] if prompt is None else outs


def _hgrn_kernel(*refs, chunk, n_heads, has_init, chunk_is_sequence):
    if has_init:
        (q_ref, f_ref, v_ref, g_ref, lb_ref, ng_ref, s0_ref, o_ref, sout_ref,
         st, b_s, k_s, qe_s, ke_s, o_s, u_s, stc_s) = refs
    else:
        (q_ref, f_ref, v_ref, g_ref, lb_ref, ng_ref, o_ref, sout_ref,
         st, b_s, k_s, qe_s, ke_s, o_s, u_s, stc_s) = refs
        s0_ref = None
    i = pl.program_id(2)
    tb, width = f_ref.shape[1], f_ref.shape[2]
    nchunk = tb // chunk
    grp = min(tb, HG_GROUP)
    levels = [w for w in (32, 16, 8) if 2 * w <= chunk]
    assert chunk % HG_BLOCK == 0 and grp % chunk == 0 and tb % grp == 0
    heads = [slice(hh * HEAD_DIM, (hh + 1) * HEAD_DIM) for hh in range(n_heads)]

    if not chunk_is_sequence:
        @pl.when(i == 0)
        def _init():
            for hh in range(n_heads):
                st[hh] = s0_ref[0, hh].T if has_init else jnp.zeros((HEAD_DIM, HEAD_DIM), F32)

    lb = lb_ref[...]
    forget = lb + (1.0 - lb) * jax.nn.sigmoid(f_ref[0])
    kk = 1.0 - forget
    b = jnp.log(forget) * LOG2E
    row = lax.broadcasted_iota(jnp.int32, b.shape, 0) % chunk
    sh = 1
    while sh < chunk:
        b = b + jnp.where(row >= sh, pltpu.roll(b, sh, axis=0), 0.0)
        sh *= 2
    b3 = b.reshape(nchunk, chunk, width)
    b_end = jnp.broadcast_to(b3[:, chunk - 1:chunk, :], b3.shape).reshape(tb, width)
    b_s[...] = b
    k_s[...] = kk
    qe_s[...] = q_ref[0] * jnp.exp2(b)
    ke_s[...] = kk * jnp.exp2(b_end - b)

    rowi = lax.broadcasted_iota(jnp.int32, (grp, HEAD_DIM), 0)
    ti = lax.broadcasted_iota(jnp.int32, (grp, grp), 0)
    si = lax.broadcasted_iota(jnp.int32, (grp, grp), 1)

    def block_row(x, period, r):
        x3 = x.reshape(grp // period, period, HEAD_DIM)
        return jnp.broadcast_to(x3[:, r:r + 1, :], x3.shape).reshape(grp, HEAD_DIM)

    def neg_unless(cond):
        return jnp.where(cond, 0.0, -jnp.inf)

    same_block = ti // HG_BLOCK == si // HG_BLOCK
    row_from = [neg_unless(rowi % HG_BLOCK >= s) for s in range(HG_BLOCK)]
    pair_at = [(same_block & (si % HG_BLOCK == s)).astype(F32) for s in range(HG_BLOCK)]
    split = [(neg_unless(rowi % (2 * w) >= w), neg_unless(rowi % (2 * w) < w),
              (ti // (2 * w) == si // (2 * w)).astype(F32)) for w in levels]

    def group_body(gi, carry):
        rs = pl.ds(pl.multiple_of(gi * grp, grp), grp)
        for cs in heads:
            qg = q_ref[0, rs, cs]
            kg = k_s[rs, cs]
            bg = b_s[rs, cs]
            kgb = kg.astype(BF16)
            a = jnp.zeros((grp, grp), F32)
            for s in range(HG_BLOCK):
                e = jnp.exp2(bg - block_row(bg, HG_BLOCK, s) + row_from[s])
                x = lax.dot_general((qg * e).astype(BF16), kgb, _NT, preferred_element_type=F32)
                a = a + x * pair_at[s]
            for w, (later, earlier, same_parent) in zip(levels, split):
                d = bg - block_row(bg, 2 * w, w)
                x = lax.dot_general((qg * jnp.exp2(d + later)).astype(BF16),
                                    (kg * jnp.exp2(earlier - d)).astype(BF16), _NT,
                                    preferred_element_type=F32)
                a = a + x * same_parent
            o_s[rs, cs] = jnp.dot(a.astype(BF16), v_ref[0, rs, cs].astype(BF16), preferred_element_type=F32)
        return carry

    lax.fori_loop(0, tb // grp, group_body, 0, unroll=math.gcd(tb // grp, 4))

    for hh, cs in enumerate(heads):
        for c in range(nchunk):
            rs = slice(c * chunk, (c + 1) * chunk)
            u_s[c] = lax.dot_general(v_ref[0, rs, cs].astype(BF16), ke_s[rs, cs].astype(BF16), _TN,
                                     preferred_element_type=F32)
        s_cur = None if chunk_is_sequence else st[hh]
        for c in range(nchunk):
            if chunk_is_sequence:
                s_cur = s0_ref[c, hh].T
            stc_s[c] = s_cur.astype(BF16)
            last = (c + 1) * chunk - 1
            s_cur = s_cur * jnp.exp2(b_s[last:last + 1, cs]) + u_s[c]
            if chunk_is_sequence:
                sout_ref[c, hh] = s_cur.T
        if not chunk_is_sequence:
            st[hh] = s_cur
        for c in range(nchunk):
            rs = slice(c * chunk, (c + 1) * chunk)
            o_s[rs, cs] += lax.dot_general(qe_s[rs, cs].astype(BF16), stc_s[c], _NT,
                                           preferred_element_type=F32)

    for cs in heads:
        o = o_s[:, cs]
        y = o * lax.rsqrt(jnp.mean(o * o, axis=-1, keepdims=True) + NORM_EPS) * ng_ref[...]
        o_ref[0, :, cs] = (y * _silu(g_ref[0, :, cs])).astype(o_ref.dtype)

    if not chunk_is_sequence:
        @pl.when(i == pl.num_programs(2) - 1)
        def _emit():
            for hh in range(n_heads):
                sout_ref[0, hh] = st[hh].T


def _hgrn(z, sec0, lb, norm_g, s0, batch, length, n_heads, heads_per_step, tb, chunk, out_dtype, seqs_per_step=1):
    width = z.shape[2]
    wblk = heads_per_step * HEAD_DIM
    has_init = s0 is not None
    chunk_is_sequence = seqs_per_step > 1
    if chunk_is_sequence:
        assert has_init and tb == chunk == length and batch % seqs_per_step == 0
        batch, length, tb = batch // seqs_per_step, length * seqs_per_step, tb * seqs_per_step
    nblk = length // tb

    def sec_spec(sec):
        return pl.BlockSpec((1, tb, wblk), lambda b, h, i: (sec, b * nblk + i, h))

    in_specs = [sec_spec(sec0), sec_spec(sec0 + 1), sec_spec(sec0 + 2), sec_spec(sec0 + 3),
                pl.BlockSpec((1, wblk), lambda b, h, i: (0, h)),
                pl.BlockSpec((1, HEAD_DIM), lambda b, h, i: (0, 0))]
    args = [z, z, z, z, lb.reshape(1, width), norm_g.reshape(1, HEAD_DIM)]
    state_spec = pl.BlockSpec((seqs_per_step, heads_per_step, HEAD_DIM, HEAD_DIM), lambda b, h, i: (b, h, 0, 0))
    if has_init:
        in_specs.append(state_spec)
        args.append(s0)
    kern = functools.partial(_hgrn_kernel, chunk=chunk, n_heads=heads_per_step, has_init=has_init,
                             chunk_is_sequence=chunk_is_sequence)
    return pl.pallas_call(
        kern,
        out_shape=(jax.ShapeDtypeStruct((1, batch * length, width), out_dtype),
                   jax.ShapeDtypeStruct((batch * seqs_per_step, n_heads, HEAD_DIM, HEAD_DIM), F32)),
        grid=(batch, n_heads // heads_per_step, nblk),
        in_specs=in_specs,
        out_specs=(pl.BlockSpec((1, tb, wblk), lambda b, h, i: (0, b * nblk + i, h)), state_spec),
        scratch_shapes=[pltpu.VMEM((heads_per_step, HEAD_DIM, HEAD_DIM), F32)]
                       + [pltpu.VMEM((tb, wblk), F32)] * 5
                       + [pltpu.VMEM((tb // chunk, HEAD_DIM, HEAD_DIM), F32),
                          pltpu.VMEM((tb // chunk, HEAD_DIM, HEAD_DIM), BF16)],
        compiler_params=_params("arbitrary", "arbitrary", "arbitrary"),
        name="hgrn",
    )(*args)


def _outproj_kernel(ma_ref, mh_ref, w_ref, x_ref, rg_ref, fg_ref, y_ref, *, final_norm):
    half = ma_ref.shape[1]
    u = jnp.dot(ma_ref[...].astype(BF16), w_ref[:half, :], preferred_element_type=F32)
    u = u + jnp.dot(mh_ref[...].astype(BF16), w_ref[half:, :], preferred_element_type=F32)
    r = x_ref[...] + rg_ref[0] * u
    if final_norm:
        r = r * lax.rsqrt(jnp.mean(r * r, axis=-1, keepdims=True) + NORM_EPS) * fg_ref[...]
    y_ref[...] = r


def _outproj(m_att, m_hg, w_bf, x, rg, final_g, tm, rows_per_mod, final_norm):
    r, d = x.shape
    half = m_att.shape[1]
    rb = rg.shape[1]
    tiles_per_mod = rows_per_mod // tm
    return pl.pallas_call(
        functools.partial(_outproj_kernel, final_norm=final_norm),
        out_shape=jax.ShapeDtypeStruct((r, d), F32),
        grid=(r // tm,),
        in_specs=[pl.BlockSpec((tm, half), lambda i: (i, 0)),
                  pl.BlockSpec((tm, half), lambda i: (i, 0)),
                  pl.BlockSpec((2 * half, d), lambda i: (0, 0)),
                  pl.BlockSpec((tm, d), lambda i: (i, 0)),
                  pl.BlockSpec((1, rb, d), lambda i: (i // tiles_per_mod, 0, 0)),
                  pl.BlockSpec((1, d), lambda i: (0, 0))],
        out_specs=pl.BlockSpec((tm, d), lambda i: (i, 0)),
        compiler_params=_params("arbitrary"),
        name="outproj",
    )(m_att, m_hg, w_bf, x, rg, final_g.reshape(1, d))


def kernel(x_prompt, x_sample, cache_k, cache_v, state_hgrn, page_table, c_prompt, c_sample,
           norm_g, w_ada, b_ada, w_in, lambda_q1, lambda_k1, lambda_q2, lambda_k2, subln_g,
           hg_lower_bounds, hg_norm_g, w_out, final_g):
    batch, seq, d = x_prompt.shape
    dec_batch, n_new, _ = x_sample.shape
    depth, n_pool, page, n_heads, _, qk_dim = cache_k.shape
    hg_heads = state_hgrn.shape[2]
    width = d // 2
    assert qk_dim == QK_DIM and n_heads * HEAD_DIM == width and hg_heads * HEAD_DIM == width
    assert w_in.shape[2] == N_SECTIONS * width

    slopes = jnp.asarray([2.0 ** (-8.0 * (h + 1) / n_heads) for h in range(n_heads)], F32)
    lb_all = jnp.cumsum(jax.nn.softmax(hg_lower_bounds.astype(F32), axis=0), axis=0)
    n_mod = batch + dec_batch
    n_mod_pad = -(-n_mod // 8) * 8
    c_all = jnp.concatenate([c_prompt, c_sample, jnp.zeros((n_mod_pad - n_mod, d), F32)], axis=0)

    xp = x_prompt.reshape(batch * seq, d)
    xs = x_sample.reshape(dec_batch * n_new, d)
    rs = dec_batch * n_new
    outs = [[] for _ in range(6)]
    for l in range(depth):
        last = l == depth - 1
        lam_init = 0.8 - 0.6 * math.exp(-0.3 * l)
        lam = (jnp.exp(jnp.sum(lambda_q1[l].astype(F32) * lambda_k1[l].astype(F32)))
               - jnp.exp(jnp.sum(lambda_q2[l].astype(F32) * lambda_k2[l].astype(F32))) + lam_init)
        par = jnp.concatenate([lam.reshape(1), slopes, jnp.zeros((7,), F32)])
        w_in_bf = w_in[l].astype(BF16)
        w_out_bf = w_out[l].astype(BF16)

        mod = _adaln(c_all, w_ada[l], b_ada[l])
        shift, scale, rgate = mod[:, :d], mod[:, d:2 * d], mod[:, 2 * d:]

        def rows_p(t):
            return t[:batch].reshape(batch, 1, d)

        def rows_s(t):
            return jnp.repeat(t[batch:n_mod], n_new, axis=0).reshape(1, rs, d)

        hp = _modnorm(xp, norm_g[l], rows_p(scale), rows_p(shift), tm=512, rows_per_mod=seq)
        qkv, gh, k_t, v_rows = _inproj_prompt(hp, w_in_bf, batch, seq, n_heads, tm=min(seq, 1024))
        hs = _modnorm(xs, norm_g[l], rows_s(scale), rows_s(shift), tm=rs, rows_per_mod=rs)
        zs = _inproj(hs, w_in_bf, tm=rs)
        cache_kt = jnp.transpose(cache_k[l], (0, 2, 3, 4, 1)).reshape(n_pool, width, page)
        cache_vr = cache_v[l].reshape(n_pool, page * n_heads, HEAD_DIM)
        tq_f, tk_f = min(seq, 512), min(seq, 512) // 2
        npg = math.gcd(page_table.shape[1], SAMPLE_PAGES_PER_STEP)
        n_stream_steps = dec_batch * (page_table.shape[1] // npg)
        fused = _attn_schedule(batch * n_heads, seq // tq_f, n_stream_steps) is not None
        att_s = _attn_sample(zs.reshape(N_SECTIONS, dec_batch, n_new, width), cache_kt, cache_vr,
                             page_table, par, subln_g[l], n_heads, n_new, 1.0 - lam_init,
                             npg=npg,
                             prompt=(qkv, gh, batch, seq, tq_f, tk_f) if fused else None)
        if fused:
            att_s, att_p = att_s
        else:
            att_p = _attn_prompt(qkv, gh, par, subln_g[l], batch, seq, n_heads, 1.0 - lam_init,
                                 tq=min(seq, 1024), tk=min(seq, 1024) // 2)
        hg_p, s_p = _hgrn(gh, 1, lb_all[l], hg_norm_g[l], None, batch, seq, hg_heads, heads_per_step=1,
                          tb=min(seq, 1024), chunk=math.gcd(seq, HG_CHUNK), out_dtype=BF16)
        xp = _outproj(att_p, hg_p[0], w_out_bf, xp, rows_p(rgate), final_g,
                      tm=512, rows_per_mod=seq, final_norm=last)
        k_p = jnp.transpose(k_t.reshape(batch, n_heads, 2, QK_DIM, seq), (0, 4, 1, 2, 3))
        outs[0].append(k_p)
        outs[1].append(v_rows.reshape(batch, seq, n_heads, HEAD_DIM))
        outs[2].append(s_p)

        hg_s, s_s = _hgrn(zs, 4, lb_all[l], hg_norm_g[l], state_hgrn[l], dec_batch, n_new, hg_heads,
                          heads_per_step=hg_heads, tb=n_new, chunk=n_new, out_dtype=F32,
                          seqs_per_step=math.gcd(dec_batch, 8))
        xs = _outproj(att_s.reshape(rs, width), hg_s[0], w_out_bf, xs, rows_s(rgate), final_g,
                      tm=rs, rows_per_mod=rs, final_norm=last)
        outs[3].append(zs[1].reshape(dec_batch, n_new, n_heads, 2, QK_DIM))
        outs[4].append(zs[2].reshape(dec_batch, n_new, n_heads, HEAD_DIM))
        outs[5].append(s_s)

    y_prompt = xp.reshape(batch, seq, d)
    y_sample = xs.reshape(dec_batch, n_new, d)
    return (y_prompt, y_sample) + tuple(jnp.stack(o) for o in outs)
```

```python
import functools
import math

import jax
import jax.numpy as jnp
from jax import lax
from jax.experimental import pallas as pl
from jax.experimental.pallas import tpu as pltpu

F32 = jnp.float32
BF16 = jnp.bfloat16

NORM_EPS = 1e-6
SUBLN_EPS = 1e-5
N_SECTIONS = 8
HEAD_DIM = 128
QK_DIM = 64
HG_CHUNK = 64
HG_BLOCK = 8
HG_GROUP = 128
LOG2E = 1.4426950408889634
SAMPLE_PAGES_PER_STEP = 16
VMEM_LIMIT_BYTES = 56 * 1024 * 1024

_NT = (((1,), (1,)), ((), ()))
_TN = (((0,), (0,)), ((), ()))


def _params(*sem):
    return pltpu.CompilerParams(dimension_semantics=sem, vmem_limit_bytes=VMEM_LIMIT_BYTES)


def _silu(x):
    return x * jax.nn.sigmoid(x)


def _adaln_kernel(c_ref, w_ref, b_ref, o_ref):
    a = _silu(c_ref[...]).astype(BF16)
    o_ref[...] = jnp.dot(a, w_ref[...].astype(BF16), preferred_element_type=F32) + b_ref[...]


def _adaln(c, w, b, tn=1024):
    m, d = c.shape
    n = w.shape[1]
    return pl.pallas_call(
        _adaln_kernel,
        out_shape=jax.ShapeDtypeStruct((m, n), F32),
        grid=(n // tn,),
        in_specs=[pl.BlockSpec((m, d), lambda j: (0, 0)),
                  pl.BlockSpec((d, tn), lambda j: (0, j)),
                  pl.BlockSpec((1, tn), lambda j: (0, j))],
        out_specs=pl.BlockSpec((m, tn), lambda j: (0, j)),
        compiler_params=_params("arbitrary"),
        name="adaln",
    )(c, w, b.reshape(1, n))


def _modnorm_kernel(x_ref, g_ref, sc_ref, sh_ref, h_ref):
    x = x_ref[...]
    y = x * lax.rsqrt(jnp.mean(x * x, axis=-1, keepdims=True) + NORM_EPS) * g_ref[...]
    h_ref[...] = (y * (1.0 + sc_ref[0]) + sh_ref[0]).astype(h_ref.dtype)


def _modnorm(x, g, scale, shift, tm, rows_per_mod):
    r, d = x.shape
    rb = scale.shape[1]
    tiles_per_mod = rows_per_mod // tm
    mod_spec = pl.BlockSpec((1, rb, d), lambda i: (i // tiles_per_mod, 0, 0))
    return pl.pallas_call(
        _modnorm_kernel,
        out_shape=jax.ShapeDtypeStruct((r, d), BF16),
        grid=(r // tm,),
        in_specs=[pl.BlockSpec((tm, d), lambda i: (i, 0)),
                  pl.BlockSpec((1, d), lambda i: (0, 0)),
                  mod_spec, mod_spec],
        out_specs=pl.BlockSpec((tm, d), lambda i: (i, 0)),
        compiler_params=_params("arbitrary"),
        name="modnorm",
    )(x, g.reshape(1, d), scale, shift)


def _inproj_kernel(h_ref, w_ref, o_ref):
    o_ref[0] = jnp.dot(h_ref[...], w_ref[...], preferred_element_type=F32)


def _inproj(h, w, tm):
    r, d = h.shape
    wsec = w.shape[1] // N_SECTIONS
    return pl.pallas_call(
        _inproj_kernel,
        out_shape=jax.ShapeDtypeStruct((N_SECTIONS, r, wsec), F32),
        grid=(r // tm, N_SECTIONS),
        in_specs=[pl.BlockSpec((tm, d), lambda i, s: (i, 0)),
                  pl.BlockSpec((d, wsec), lambda i, s: (0, s))],
        out_specs=pl.BlockSpec((1, tm, wsec), lambda i, s: (s, i, 0)),
        compiler_params=_params("arbitrary", "arbitrary"),
        name="inproj",
    )(h, w)


def _inproj_prompt_kernel(h_ref, w_ref, qkv_ref, gh_ref, kt_ref, vo_ref, *, n_heads):
    s = pl.program_id(1)
    tm = h_ref.shape[0]
    z = jnp.dot(h_ref[...], w_ref[...], preferred_element_type=F32)

    @pl.when(s <= 2)
    def _():
        qkv_ref[0] = z.astype(BF16)

    @pl.when(s == 1)
    def _():
        kt_ref[0] = z.T

    @pl.when(s == 2)
    def _():
        for hh in range(n_heads):
            vo_ref[pl.ds(hh, tm, stride=n_heads), :] = z[:, hh * HEAD_DIM:(hh + 1) * HEAD_DIM]

    @pl.when(s >= 3)
    def _():
        gh_ref[0] = z


def _inproj_prompt(h, w, batch, seq, n_heads, tm):
    r, d = h.shape
    wsec = w.shape[1] // N_SECTIONS
    tpb = seq // tm
    kern = functools.partial(_inproj_prompt_kernel, n_heads=n_heads)
    return pl.pallas_call(
        kern,
        out_shape=(jax.ShapeDtypeStruct((3, r, wsec), BF16),
                   jax.ShapeDtypeStruct((N_SECTIONS - 3, r, wsec), F32),
                   jax.ShapeDtypeStruct((batch, wsec, seq), F32),
                   jax.ShapeDtypeStruct((r * n_heads, HEAD_DIM), F32)),
        grid=(r // tm, N_SECTIONS),
        in_specs=[pl.BlockSpec((tm, d), lambda i, s: (i, 0)),
                  pl.BlockSpec((d, wsec), lambda i, s: (0, s))],
        out_specs=(pl.BlockSpec((1, tm, wsec), lambda i, s: (jnp.minimum(s, 2), i, 0)),
                   pl.BlockSpec((1, tm, wsec), lambda i, s: (jnp.maximum(s, 3) - 3, i, 0)),
                   pl.BlockSpec((1, wsec, tm), lambda i, s: (i // tpb, 0, i % tpb)),
                   pl.BlockSpec((tm * n_heads, HEAD_DIM), lambda i, s: (i, 0))),
        compiler_params=_params("arbitrary", "arbitrary"),
        name="inproj_prompt",
    )(h, w)


def _prompt_attn_ops(par_ref, q_ref, k_ref, v_ref, g_ref, sg_ref, o_ref,
                     vt, kx_s, mask_s, sx, sy, m_s, acc_s, *, h, qt, tq, tk, out_scale):
    lam = par_ref[0]
    slope2 = par_ref[1 + h] * LOG2E
    assert tq == 2 * tk

    def fill():
        ones_row = (lax.broadcasted_iota(jnp.int32, (vt.shape[1] - HEAD_DIM, tk), 0) == 0).astype(BF16)
        for j in range(k_ref.shape[1] // tk):
            vt[j, :HEAD_DIM, :] = v_ref[0, j * tk:(j + 1) * tk, :].astype(F32).T.astype(BF16)
            vt[j, HEAD_DIM:, :] = ones_row
        jl = lax.broadcasted_iota(jnp.int32, (tk, HEAD_DIM), 0).astype(F32) * slope2
        hi = jl.astype(BF16).astype(F32)
        mid = (jl - hi).astype(BF16).astype(F32)
        lo = jl - hi - mid
        lane = lax.broadcasted_iota(jnp.int32, (tk, HEAD_DIM), 1)
        kx_s[...] = jnp.where(lane == 0, hi, jnp.where(lane == 1, mid, jnp.where(lane == 2, lo, 0.0))).astype(BF16)
        jj = lax.broadcasted_iota(jnp.int32, (tk, tq), 0)
        ii = lax.broadcasted_iota(jnp.int32, (tk, tq), 1)
        mask_s[...] = jnp.where(ii >= jj, 0.0, -jnp.inf)

    q = q_ref[0].astype(F32) * (QK_DIM ** -0.5 * LOG2E)
    lane = lax.broadcasted_iota(jnp.int32, q.shape, 1)
    qx = (lane < 3).astype(F32).astype(BF16)
    qs = (jnp.concatenate([jnp.where(lane < QK_DIM, q, 0.0).astype(BF16), qx], axis=1),
          jnp.concatenate([jnp.where(lane >= QK_DIM, q, 0.0).astype(BF16), qx], axis=1))
    c0 = -slope2 * lax.broadcasted_iota(jnp.int32, (1, tq), 1).astype(F32)

    whole = slice(0, tq)

    def qk(kt, j, cols=whole):
        keys = k_ref[0, pl.ds(pl.multiple_of(kt * tk, tk), tk), :]
        return lax.dot_general(jnp.concatenate([keys, kx_s[...]], axis=1), qs[j][cols], _NT,
                               preferred_element_type=F32)

    def update(j, s, kt, cols=whole):
        c = c0[:, cols] + slope2 * (jnp.zeros((1, s.shape[1]), jnp.int32) + (kt * tk - qt * tq)).astype(F32)
        m_old = m_s[j, :, cols]
        m_new = jnp.maximum(m_old, jnp.max(s, axis=0, keepdims=True) + c)
        alpha = jnp.exp2(m_old - m_new)
        p = jnp.exp2(s - (m_new - c))
        m_s[j, :, cols] = m_new
        acc_s[j, :, cols] = alpha * acc_s[j, :, cols] + jnp.dot(vt[kt], p.astype(BF16),
                                                                preferred_element_type=F32)

    def start():
        m_s[...] = jnp.full(m_s.shape, -jnp.inf, F32)
        acc_s[...] = jnp.zeros(acc_s.shape, F32)
        for j in range(2):
            sx[j] = qk(0, j)

    def step(u):
        ka = 2 * u
        for j in range(2):
            sy[j] = qk(ka + 1, j)
        for j in range(2):
            update(j, sx[j], ka)
        for j in range(2):
            sx[j] = qk(ka + 2, j)
        for j in range(2):
            update(j, sy[j], ka + 1)

    def finish():
        kd = 2 * qt
        late = slice(tk, tq)
        for j in range(2):
            sy[j, :, late] = qk(kd + 1, j, late)
        for j in range(2):
            update(j, sx[j] + mask_s[...], kd)
        for j in range(2):
            update(j, sy[j, :, late] + mask_s[:, :tk], kd + 1, late)
        o0 = acc_s[0]
        o1 = acc_s[1]
        od = (o0[:HEAD_DIM] / o0[HEAD_DIM:HEAD_DIM + 1] - lam * (o1[:HEAD_DIM] / o1[HEAD_DIM:HEAD_DIM + 1])).T
        y = od * lax.rsqrt(jnp.mean(od * od, axis=-1, keepdims=True) + SUBLN_EPS) * sg_ref[...] * out_scale
        o_ref[...] = (y * _silu(g_ref[0])).astype(o_ref.dtype)

    return fill, start, step, finish


def _prompt_attn_scratch(seq, tq, tk):
    return [pltpu.VMEM((seq // tk, HEAD_DIM + 16, tk), BF16),
            pltpu.VMEM((tk, HEAD_DIM), BF16),
            pltpu.VMEM((tk, tq), F32),
            pltpu.VMEM((2, tk, tq), F32),
            pltpu.VMEM((2, tk, tq), F32),
            pltpu.VMEM((2, 1, tq), F32),
            pltpu.VMEM((2, HEAD_DIM + 16, tq), F32)]


def _attn_prompt_kernel(*refs, tq, tk, out_scale):
    qt = pl.program_id(2)
    fill, start, step, finish = _prompt_attn_ops(*refs, h=pl.program_id(1), qt=qt, tq=tq, tk=tk,
                                                 out_scale=out_scale)
    pl.when(qt == 0)(fill)
    start()
    lax.fori_loop(0, qt, lambda u, c: (step(u), c)[1], 0)
    finish()


def _attn_prompt(qkv, gh, par, subln_g, batch, seq, n_heads, out_scale, tq=1024, tk=512):
    nq = seq // tq
    kern = functools.partial(_attn_prompt_kernel, tq=tq, tk=tk, out_scale=out_scale)
    return pl.pallas_call(
        kern,
        out_shape=jax.ShapeDtypeStruct((batch * seq, n_heads * HEAD_DIM), BF16),
        grid=(batch, n_heads, nq),
        in_specs=[pl.BlockSpec(memory_space=pltpu.SMEM),
                  pl.BlockSpec((1, tq, HEAD_DIM), lambda b, h, i: (0, b * nq + i, h)),
                  pl.BlockSpec((1, seq, HEAD_DIM), lambda b, h, i: (1, b, h)),
                  pl.BlockSpec((1, seq, HEAD_DIM), lambda b, h, i: (2, b, h)),
                  pl.BlockSpec((1, tq, HEAD_DIM), lambda b, h, i: (0, b * nq + i, h)),
                  pl.BlockSpec((1, HEAD_DIM), lambda b, h, i: (0, 0))],
        out_specs=pl.BlockSpec((tq, HEAD_DIM), lambda b, h, i: (b * nq + i, h)),
        scratch_shapes=_prompt_attn_scratch(seq, tq, tk),
        compiler_params=_params("arbitrary", "arbitrary", "arbitrary"),
        name="attn_prompt",
    )(par, qkv, qkv, qkv, gh, subln_g.reshape(1, HEAD_DIM))


def _attn_sample_kernel(pt_ref, par_ref, slope_ref, q_ref, kn_ref, vn_ref, g_ref, sg_ref, *rest,
                        npg, page, n_new, n_heads, out_scale):
    k_refs = rest[:npg]
    v_refs = rest[npg:2 * npg]
    o_ref = rest[2 * npg]
    qbd, b0_s, m_s, l_s, acc_s = rest[2 * npg + 1:]
    i = pl.program_id(1)
    n_steps = pl.num_programs(1)
    tblk = npg * page
    nrow, width = qbd.shape
    rows_per_head = 2 * n_new
    slope = slope_ref[...] * LOG2E

    def head_rows(hh):
        return slice(hh * rows_per_head, (hh + 1) * rows_per_head)

    @pl.when(i == 0)
    def _init():
        q = q_ref[0] * (QK_DIM ** -0.5 * LOG2E)
        qt = jnp.concatenate([q] * (nrow // n_new), axis=0)
        rr = lax.broadcasted_iota(jnp.int32, (nrow, width), 0)
        cc = lax.broadcasted_iota(jnp.int32, (nrow, width), 1)
        qbd[...] = jnp.where(rr // n_new == cc // QK_DIM, qt, 0.0).astype(BF16)
        b0_s[...] = slope * lax.broadcasted_iota(jnp.int32, (nrow, tblk), 1).astype(F32)
        zpad = jnp.zeros((nrow - n_new, width), F32)
        kn = jnp.concatenate([kn_ref[0], zpad], axis=0).astype(BF16)
        vn = jnp.concatenate([vn_ref[0], zpad], axis=0).astype(BF16)
        s = lax.dot_general(qbd[...], kn, _NT, preferred_element_type=F32)
        tk = lax.broadcasted_iota(jnp.int32, (nrow, nrow), 1)
        tqry = lax.broadcasted_iota(jnp.int32, (nrow, nrow), 0) % n_new
        s = jnp.where(tk <= tqry, s + slope * tk.astype(F32), -jnp.inf)
        m = jnp.max(s, axis=1, keepdims=True)
        p = jnp.exp2(s - m)
        m_s[...] = m
        l_s[...] = jnp.sum(p, axis=1, keepdims=True)
        pb = p.astype(BF16)
        for hh in range(n_heads):
            acc_s[head_rows(hh), :] = jnp.dot(pb[head_rows(hh), :], vn[:, hh * HEAD_DIM:(hh + 1) * HEAD_DIM],
                                              preferred_element_type=F32)

    gp = 2 if npg % 2 == 0 else 1
    qb = qbd[...]
    s = jnp.concatenate(
        [jnp.dot(qb, jnp.concatenate([k_refs[pg + t][...].astype(BF16) for t in range(gp)], axis=1),
                 preferred_element_type=F32) for pg in range(0, npg, gp)], axis=1) + b0_s[...]
    c = slope * (jnp.zeros((nrow, 1), jnp.int32) + (i - n_steps) * tblk).astype(F32)
    m_old = m_s[...]
    m_new = jnp.maximum(m_old, jnp.max(s, axis=1, keepdims=True) + c)
    alpha = jnp.exp2(m_old - m_new)
    p = jnp.exp2(s - (m_new - c))
    m_s[...] = m_new
    l_s[...] = alpha * l_s[...] + jnp.sum(p, axis=1, keepdims=True)
    pb = p.astype(BF16)
    for hh in range(n_heads):
        rs = head_rows(hh)
        pv = jnp.zeros((rows_per_head, HEAD_DIM), F32)
        for pg in range(0, npg, gp):
            vb = jnp.concatenate([v_refs[pg + t][pl.ds(hh, page, stride=n_heads), :].astype(BF16)
                                  for t in range(gp)], axis=0)
            pv = pv + jnp.dot(pb[rs, pg * page:(pg + gp) * page], vb, preferred_element_type=F32)
        acc_s[rs, :] = alpha[rs, :] * acc_s[rs, :] + pv

    @pl.when(i == n_steps - 1)
    def _finish():
        lam = par_ref[0]
        o = acc_s[...] / l_s[...]
        g = g_ref[0]
        for hh in range(n_heads):
            cs = slice(hh * HEAD_DIM, (hh + 1) * HEAD_DIM)
            blk = o[head_rows(hh), :]
            od = blk[:n_new] - lam * blk[n_new:]
            y = od * lax.rsqrt(jnp.mean(od * od, axis=-1, keepdims=True) + SUBLN_EPS) * sg_ref[...] * out_scale
            o_ref[0, :, cs] = y * _silu(g[:, cs])

def _attn_sample(z, cache_kt, cache_v, page_table, par, subln_g, n_heads, n_new, out_scale, npg):
    dec_batch, n_pages = page_table.shape
    width, page = cache_kt.shape[1], cache_kt.shape[2]
    nrow = n_heads * 2 * n_new
    assert nrow == HEAD_DIM and n_pages % npg == 0
    tblk = npg * page
    n_steps = n_pages // npg
    row_head = jnp.arange(nrow) // (2 * n_new)
    slope_col = jnp.exp2(-8.0 * (row_head + 1).astype(F32) / n_heads).reshape(nrow, 1)

    def page_spec(pg, rows, cols):
        return pl.BlockSpec((None, rows, cols), lambda s, i, pt: (pt[s, i * npg + pg], 0, 0))

    def sec_spec(sec):
        return pl.BlockSpec((1, None, n_new, width), lambda s, i, pt: (sec, s, 0, 0))

    kern = functools.partial(_attn_sample_kernel, npg=npg, page=page, n_new=n_new, n_heads=n_heads,
                             out_scale=out_scale)
    grid_spec = pltpu.PrefetchScalarGridSpec(
        num_scalar_prefetch=1,
        grid=(dec_batch, n_steps),
        in_specs=[pl.BlockSpec(memory_space=pltpu.SMEM),
                  pl.BlockSpec((nrow, 1), lambda s, i, pt: (0, 0)),
                  sec_spec(0), sec_spec(1), sec_spec(2), sec_spec(3),
                  pl.BlockSpec((1, HEAD_DIM), lambda s, i, pt: (0, 0))]
                 + [page_spec(pg, width, page) for pg in range(npg)]
                 + [page_spec(pg, page * n_heads, HEAD_DIM) for pg in range(npg)],
        out_specs=pl.BlockSpec((1, n_new, width), lambda s, i, pt: (s, 0, 0)),
        scratch_shapes=[pltpu.VMEM((nrow, width), BF16),
                        pltpu.VMEM((nrow, tblk), F32),
                        pltpu.VMEM((nrow, 1), F32),
                        pltpu.VMEM((nrow, 1), F32),
                        pltpu.VMEM((nrow, HEAD_DIM), F32)],
    )
    return pl.pallas_call(
        kern,
        out_shape=jax.ShapeDtypeStruct((dec_batch, n_new, width), F32),
        grid_spec=grid_spec,
        compiler_params=_params("arbitrary", "arbitrary"),
        name="attn_sample",
    )(page_table, par, slope_col, z, z, z, z, subln_g.reshape(1, HEAD_DIM),
      *([cache_kt] * npg), *([cache_v] * npg))


def _hgrn_kernel(*refs, chunk, n_heads, has_init, chunk_is_sequence):
    if has_init:
        (q_ref, f_ref, v_ref, g_ref, lb_ref, ng_ref, s0_ref, o_ref, sout_ref,
         st, b_s, k_s, qe_s, ke_s, o_s, u_s, stc_s) = refs
    else:
        (q_ref, f_ref, v_ref, g_ref, lb_ref, ng_ref, o_ref, sout_ref,
         st, b_s, k_s, qe_s, ke_s, o_s, u_s, stc_s) = refs
        s0_ref = None
    i = pl.program_id(2)
    tb, width = f_ref.shape[1], f_ref.shape[2]
    nchunk = tb // chunk
    grp = min(tb, HG_GROUP)
    levels = [w for w in (32, 16, 8) if 2 * w <= chunk]
    assert chunk % HG_BLOCK == 0 and grp % chunk == 0 and tb % grp == 0
    heads = [slice(hh * HEAD_DIM, (hh + 1) * HEAD_DIM) for hh in range(n_heads)]

    if not chunk_is_sequence:
        @pl.when(i == 0)
        def _init():
            for hh in range(n_heads):
                st[hh] = s0_ref[0, hh].T if has_init else jnp.zeros((HEAD_DIM, HEAD_DIM), F32)

    lb = lb_ref[...]
    forget = lb + (1.0 - lb) * jax.nn.sigmoid(f_ref[0])
    kk = 1.0 - forget
    b = jnp.log(forget) * LOG2E
    row = lax.broadcasted_iota(jnp.int32, b.shape, 0) % chunk
    sh = 1
    while sh < chunk:
        b = b + jnp.where(row >= sh, pltpu.roll(b, sh, axis=0), 0.0)
        sh *= 2
    b3 = b.reshape(nchunk, chunk, width)
    b_end = jnp.broadcast_to(b3[:, chunk - 1:chunk, :], b3.shape).reshape(tb, width)
    b_s[...] = b
    k_s[...] = kk
    qe_s[...] = q_ref[0] * jnp.exp2(b)
    ke_s[...] = kk * jnp.exp2(b_end - b)

    rowi = lax.broadcasted_iota(jnp.int32, (grp, HEAD_DIM), 0)
    ti = lax.broadcasted_iota(jnp.int32, (grp, grp), 0)
    si = lax.broadcasted_iota(jnp.int32, (grp, grp), 1)

    def block_row(x, period, r):
        x3 = x.reshape(grp // period, period, HEAD_DIM)
        return jnp.broadcast_to(x3[:, r:r + 1, :], x3.shape).reshape(grp, HEAD_DIM)

    def neg_unless(cond):
        return jnp.where(cond, 0.0, -jnp.inf)

    same_block = ti // HG_BLOCK == si // HG_BLOCK
    row_from = [neg_unless(rowi % HG_BLOCK >= s) for s in range(HG_BLOCK)]
    pair_at = [(same_block & (si % HG_BLOCK == s)).astype(F32) for s in range(HG_BLOCK)]
    split = [(neg_unless(rowi % (2 * w) >= w), neg_unless(rowi % (2 * w) < w),
              (ti // (2 * w) == si // (2 * w)).astype(F32)) for w in levels]

    def group_body(gi, carry):
        rs = pl.ds(pl.multiple_of(gi * grp, grp), grp)
        for cs in heads:
            qg = q_ref[0, rs, cs]
            kg = k_s[rs, cs]
            bg = b_s[rs, cs]
            kgb = kg.astype(BF16)
            a = jnp.zeros((grp, grp), F32)
            for s in range(HG_BLOCK):
                e = jnp.exp2(bg - block_row(bg, HG_BLOCK, s) + row_from[s])
                x = lax.dot_general((qg * e).astype(BF16), kgb, _NT, preferred_element_type=F32)
                a = a + x * pair_at[s]
            for w, (later, earlier, same_parent) in zip(levels, split):
                d = bg - block_row(bg, 2 * w, w)
                x = lax.dot_general((qg * jnp.exp2(d + later)).astype(BF16),
                                    (kg * jnp.exp2(earlier - d)).astype(BF16), _NT,
                                    preferred_element_type=F32)
                a = a + x * same_parent
            o_s[rs, cs] = jnp.dot(a.astype(BF16), v_ref[0, rs, cs].astype(BF16), preferred_element_type=F32)
        return carry

    lax.fori_loop(0, tb // grp, group_body, 0, unroll=math.gcd(tb // grp, 4))

    for hh, cs in enumerate(heads):
        for c in range(nchunk):
            rs = slice(c * chunk, (c + 1) * chunk)
            u_s[c] = lax.dot_general(v_ref[0, rs, cs].astype(BF16), ke_s[rs, cs].astype(BF16), _TN,
                                     preferred_element_type=F32)
        s_cur = None if chunk_is_sequence else st[hh]
        for c in range(nchunk):
            if chunk_is_sequence:
                s_cur = s0_ref[c, hh].T
            stc_s[c] = s_cur.astype(BF16)
            last = (c + 1) * chunk - 1
            s_cur = s_cur * jnp.exp2(b_s[last:last + 1, cs]) + u_s[c]
            if chunk_is_sequence:
                sout_ref[c, hh] = s_cur.T
        if not chunk_is_sequence:
            st[hh] = s_cur
        for c in range(nchunk):
            rs = slice(c * chunk, (c + 1) * chunk)
            o_s[rs, cs] += lax.dot_general(qe_s[rs, cs].astype(BF16), stc_s[c], _NT,
                                           preferred_element_type=F32)

    for cs in heads:
        o = o_s[:, cs]
        y = o * lax.rsqrt(jnp.mean(o * o, axis=-1, keepdims=True) + NORM_EPS) * ng_ref[...]
        o_ref[0, :, cs] = (y * _silu(g_ref[0, :, cs])).astype(o_ref.dtype)

    if not chunk_is_sequence:
        @pl.when(i == pl.num_programs(2) - 1)
        def _emit():
            for hh in range(n_heads):
                sout_ref[0, hh] = st[hh].T


def _hgrn(z, sec0, lb, norm_g, s0, batch, length, n_heads, heads_per_step, tb, chunk, out_dtype, seqs_per_step=1):
    width = z.shape[2]
    wblk = heads_per_step * HEAD_DIM
    has_init = s0 is not None
    chunk_is_sequence = seqs_per_step > 1
    if chunk_is_sequence:
        assert has_init and tb == chunk == length and batch % seqs_per_step == 0
        batch, length, tb = batch // seqs_per_step, length * seqs_per_step, tb * seqs_per_step
    nblk = length // tb

    def sec_spec(sec):
        return pl.BlockSpec((1, tb, wblk), lambda b, h, i: (sec, b * nblk + i, h))

    in_specs = [sec_spec(sec0), sec_spec(sec0 + 1), sec_spec(sec0 + 2), sec_spec(sec0 + 3),
                pl.BlockSpec((1, wblk), lambda b, h, i: (0, h)),
                pl.BlockSpec((1, HEAD_DIM), lambda b, h, i: (0, 0))]
    args = [z, z, z, z, lb.reshape(1, width), norm_g.reshape(1, HEAD_DIM)]
    state_spec = pl.BlockSpec((seqs_per_step, heads_per_step, HEAD_DIM, HEAD_DIM), lambda b, h, i: (b, h, 0, 0))
    if has_init:
        in_specs.append(state_spec)
        args.append(s0)
    kern = functools.partial(_hgrn_kernel, chunk=chunk, n_heads=heads_per_step, has_init=has_init,
                             chunk_is_sequence=chunk_is_sequence)
    return pl.pallas_call(
        kern,
        out_shape=(jax.ShapeDtypeStruct((1, batch * length, width), out_dtype),
                   jax.ShapeDtypeStruct((batch * seqs_per_step, n_heads, HEAD_DIM, HEAD_DIM), F32)),
        grid=(batch, n_heads // heads_per_step, nblk),
        in_specs=in_specs,
        out_specs=(pl.BlockSpec((1, tb, wblk), lambda b, h, i: (0, b * nblk + i, h)), state_spec),
        scratch_shapes=[pltpu.VMEM((heads_per_step, HEAD_DIM, HEAD_DIM), F32)]
                       + [pltpu.VMEM((tb, wblk), F32)] * 5
                       + [pltpu.VMEM((tb // chunk, HEAD_DIM, HEAD_DIM), F32),
                          pltpu.VMEM((tb // chunk, HEAD_DIM, HEAD_DIM), BF16)],
        compiler_params=_params("arbitrary", "arbitrary", "arbitrary"),
        name="hgrn",
    )(*args)


def _outproj_kernel(ma_ref, mh_ref, w_ref, x_ref, rg_ref, fg_ref, y_ref, *, final_norm):
    half = ma_ref.shape[1]
    u = jnp.dot(ma_ref[...].astype(BF16), w_ref[:half, :], preferred_element_type=F32)
    u = u + jnp.dot(mh_ref[...].astype(BF16), w_ref[half:, :], preferred_element_type=F32)
    r = x_ref[...] + rg_ref[0] * u
    if final_norm:
        r = r * lax.rsqrt(jnp.mean(r * r, axis=-1, keepdims=True) + NORM_EPS) * fg_ref[...]
    y_ref[...] = r


def _outproj(m_att, m_hg, w_bf, x, rg, final_g, tm, rows_per_mod, final_norm):
    r, d = x.shape
    half = m_att.shape[1]
    rb = rg.shape[1]
    tiles_per_mod = rows_per_mod // tm
    return pl.pallas_call(
        functools.partial(_outproj_kernel, final_norm=final_norm),
        out_shape=jax.ShapeDtypeStruct((r, d), F32),
        grid=(r // tm,),
        in_specs=[pl.BlockSpec((tm, half), lambda i: (i, 0)),
                  pl.BlockSpec((tm, half), lambda i: (i, 0)),
                  pl.BlockSpec((2 * half, d), lambda i: (0, 0)),
                  pl.BlockSpec((tm, d), lambda i: (i, 0)),
                  pl.BlockSpec((1, rb, d), lambda i: (i // tiles_per_mod, 0, 0)),
                  pl.BlockSpec((1, d), lambda i: (0, 0))],
        out_specs=pl.BlockSpec((tm, d), lambda i: (i, 0)),
        compiler_params=_params("arbitrary"),
        name="outproj",
    )(m_att, m_hg, w_bf, x, rg, final_g.reshape(1, d))


def _tile_plan(seq, n_pages):
    return {"rows": math.gcd(seq, 1024), "out_rows": math.gcd(seq, 512),
            "pages": math.gcd(n_pages, SAMPLE_PAGES_PER_STEP)}


def kernel(x_prompt, x_sample, cache_k, cache_v, state_hgrn, page_table, c_prompt, c_sample,
           norm_g, w_ada, b_ada, w_in, lambda_q1, lambda_k1, lambda_q2, lambda_k2, subln_g,
           hg_lower_bounds, hg_norm_g, w_out, final_g):
    batch, seq, d = x_prompt.shape
    dec_batch, n_new, _ = x_sample.shape
    depth, n_pool, page, n_heads, _, qk_dim = cache_k.shape
    hg_heads = state_hgrn.shape[2]
    width = d // 2
    assert qk_dim == QK_DIM and n_heads * HEAD_DIM == width and hg_heads * HEAD_DIM == width
    assert w_in.shape[2] == N_SECTIONS * width
    tiles = _tile_plan(seq, page_table.shape[1])

    slopes = jnp.asarray([2.0 ** (-8.0 * (h + 1) / n_heads) for h in range(n_heads)], F32)
    lb_all = jnp.cumsum(jax.nn.softmax(hg_lower_bounds.astype(F32), axis=0), axis=0)
    n_mod = batch + dec_batch
    n_mod_pad = -(-n_mod // 8) * 8
    c_all = jnp.concatenate([c_prompt, c_sample, jnp.zeros((n_mod_pad - n_mod, d), F32)], axis=0)

    xp = x_prompt.reshape(batch * seq, d)
    xs = x_sample.reshape(dec_batch * n_new, d)
    rs = dec_batch * n_new
    outs = [[] for _ in range(6)]
    for l in range(depth):
        last = l == depth - 1
        lam_init = 0.8 - 0.6 * math.exp(-0.3 * l)
        lam = (jnp.exp(jnp.sum(lambda_q1[l].astype(F32) * lambda_k1[l].astype(F32)))
               - jnp.exp(jnp.sum(lambda_q2[l].astype(F32) * lambda_k2[l].astype(F32))) + lam_init)
        par = jnp.concatenate([lam.reshape(1), slopes, jnp.zeros((7,), F32)])
        w_in_bf = w_in[l].astype(BF16)
        w_out_bf = w_out[l].astype(BF16)

        mod = _adaln(c_all, w_ada[l], b_ada[l])
        shift, scale, rgate = mod[:, :d], mod[:, d:2 * d], mod[:, 2 * d:]

        def rows_p(t):
            return t[:batch].reshape(batch, 1, d)

        def rows_s(t):
            return jnp.repeat(t[batch:n_mod], n_new, axis=0).reshape(1, rs, d)

        t = tiles
        hp = _modnorm(xp, norm_g[l], rows_p(scale), rows_p(shift), tm=t["rows"], rows_per_mod=seq)
        qkv, gh, k_t, v_rows = _inproj_prompt(hp, w_in_bf, batch, seq, n_heads, tm=t["rows"])
        hs = _modnorm(xs, norm_g[l], rows_s(scale), rows_s(shift), tm=rs, rows_per_mod=rs)
        zs = _inproj(hs, w_in_bf, tm=rs)
        cache_kt = jnp.transpose(cache_k[l], (0, 2, 3, 4, 1)).reshape(n_pool, width, page)
        cache_vr = cache_v[l].reshape(n_pool, page * n_heads, HEAD_DIM)
        att_s = _attn_sample(zs.reshape(N_SECTIONS, dec_batch, n_new, width), cache_kt, cache_vr,
                             page_table, par, subln_g[l], n_heads, n_new, 1.0 - lam_init, npg=t["pages"])
        att_p = _attn_prompt(qkv, gh, par, subln_g[l], batch, seq, n_heads, 1.0 - lam_init,
                             tq=t["rows"], tk=t["rows"] // 2)
        hg_p, s_p = _hgrn(gh, 1, lb_all[l], hg_norm_g[l], None, batch, seq, hg_heads, heads_per_step=1,
                          tb=t["rows"], chunk=math.gcd(seq, HG_CHUNK), out_dtype=BF16)
        xp = _outproj(att_p, hg_p[0], w_out_bf, xp, rows_p(rgate), final_g,
                      tm=t["out_rows"], rows_per_mod=seq, final_norm=last)
        k_p = jnp.transpose(k_t.reshape(batch, n_heads, 2, QK_DIM, seq), (0, 4, 1, 2, 3))
        outs[0].append(k_p)
        outs[1].append(v_rows.reshape(batch, seq, n_heads, HEAD_DIM))
        outs[2].append(s_p)

        hg_s, s_s = _hgrn(zs, 4, lb_all[l], hg_norm_g[l], state_hgrn[l], dec_batch, n_new, hg_heads,
                          heads_per_step=hg_heads, tb=n_new, chunk=n_new, out_dtype=F32,
                          seqs_per_step=math.gcd(dec_batch, 8))
        xs = _outproj(att_s.reshape(rs, width), hg_s[0], w_out_bf, xs, rows_s(rgate), final_g,
                      tm=rs, rows_per_mod=rs, final_norm=last)
        outs[3].append(zs[1].reshape(dec_batch, n_new, n_heads, 2, QK_DIM))
        outs[4].append(zs[2].reshape(dec_batch, n_new, n_heads, HEAD_DIM))
        outs[5].append(s_s)

    y_prompt = xp.reshape(batch, seq, d)
    y_sample = xs.reshape(dec_batch, n_new, d)
    return (y_prompt, y_sample) + tuple(jnp.stack(o) for o in outs)
```

```python
import functools
import math

import jax
import jax.numpy as jnp
from jax import lax
from jax.experimental import pallas as pl
from jax.experimental.pallas import tpu as pltpu

F32 = jnp.float32
BF16 = jnp.bfloat16

NORM_EPS = 1e-6
SUBLN_EPS = 1e-5
N_SECTIONS = 8
HEAD_DIM = 128
QK_DIM = 64
HG_CHUNK = 64
HG_BLOCK = 8
HG_GROUP = 128
LOG2E = 1.4426950408889634
SAMPLE_PAGES_PER_STEP = 16
VMEM_LIMIT_BYTES = 56 * 1024 * 1024

_NT = (((1,), (1,)), ((), ()))
_TN = (((0,), (0,)), ((), ()))


def _params(*sem):
    return pltpu.CompilerParams(dimension_semantics=sem, vmem_limit_bytes=VMEM_LIMIT_BYTES)


def _silu(x):
    return x * jax.nn.sigmoid(x)


def _adaln_kernel(c_ref, w_ref, b_ref, o_ref):
    a = _silu(c_ref[...]).astype(BF16)
    o_ref[...] = jnp.dot(a, w_ref[...].astype(BF16), preferred_element_type=F32) + b_ref[...]


def _adaln(c, w, b, tn=1024):
    m, d = c.shape
    n = w.shape[1]
    return pl.pallas_call(
        _adaln_kernel,
        out_shape=jax.ShapeDtypeStruct((m, n), F32),
        grid=(n // tn,),
        in_specs=[pl.BlockSpec((m, d), lambda j: (0, 0)),
                  pl.BlockSpec((d, tn), lambda j: (0, j)),
                  pl.BlockSpec((1, tn), lambda j: (0, j))],
        out_specs=pl.BlockSpec((m, tn), lambda j: (0, j)),
        compiler_params=_params("arbitrary"),
        name="adaln",
    )(c, w, b.reshape(1, n))


def _modnorm_kernel(x_ref, g_ref, sc_ref, sh_ref, h_ref):
    x = x_ref[...]
    y = x * lax.rsqrt(jnp.mean(x * x, axis=-1, keepdims=True) + NORM_EPS) * g_ref[...]
    h_ref[...] = (y * (1.0 + sc_ref[0]) + sh_ref[0]).astype(h_ref.dtype)


def _modnorm(x, g, scale, shift, tm, rows_per_mod):
    r, d = x.shape
    rb = scale.shape[1]
    tiles_per_mod = rows_per_mod // tm
    mod_spec = pl.BlockSpec((1, rb, d), lambda i: (i // tiles_per_mod, 0, 0))
    return pl.pallas_call(
        _modnorm_kernel,
        out_shape=jax.ShapeDtypeStruct((r, d), BF16),
        grid=(r // tm,),
        in_specs=[pl.BlockSpec((tm, d), lambda i: (i, 0)),
                  pl.BlockSpec((1, d), lambda i: (0, 0)),
                  mod_spec, mod_spec],
        out_specs=pl.BlockSpec((tm, d), lambda i: (i, 0)),
        compiler_params=_params("arbitrary"),
        name="modnorm",
    )(x, g.reshape(1, d), scale, shift)


def _inproj_kernel(h_ref, w_ref, o_ref, wb_ref):
    wb = w_ref[...].astype(BF16)
    wb_ref[...] = wb
    o_ref[0] = jnp.dot(h_ref[...], wb, preferred_element_type=F32)


def _inproj(h, w):
    r, d = h.shape
    wsec = w.shape[1] // N_SECTIONS
    return pl.pallas_call(
        _inproj_kernel,
        out_shape=(jax.ShapeDtypeStruct((N_SECTIONS, r, wsec), F32),
                   jax.ShapeDtypeStruct(w.shape, BF16)),
        grid=(N_SECTIONS,),
        in_specs=[pl.BlockSpec((r, d), lambda s: (0, 0)),
                  pl.BlockSpec((d, wsec), lambda s: (0, s))],
        out_specs=(pl.BlockSpec((1, r, wsec), lambda s: (s, 0, 0)),
                   pl.BlockSpec((d, wsec), lambda s: (0, s))),
        compiler_params=_params("arbitrary"),
        name="inproj",
    )(h, w)


def _inproj_prompt_kernel(h_ref, w_ref, qkv_ref, gh_ref, kt_ref, vo_ref, *, n_heads):
    s = pl.program_id(1)
    tm = h_ref.shape[0]
    z = jnp.dot(h_ref[...], w_ref[...], preferred_element_type=F32)

    @pl.when(s <= 2)
    def _():
        qkv_ref[0] = z.astype(BF16)

    @pl.when(s == 1)
    def _():
        kt_ref[0] = z.T

    @pl.when(s == 2)
    def _():
        for hh in range(n_heads):
            vo_ref[pl.ds(hh, tm, stride=n_heads), :] = z[:, hh * HEAD_DIM:(hh + 1) * HEAD_DIM]

    @pl.when(s >= 3)
    def _():
        gh_ref[0] = z


def _inproj_prompt(h, w, batch, seq, n_heads, tm):
    r, d = h.shape
    wsec = w.shape[1] // N_SECTIONS
    tpb = seq // tm
    kern = functools.partial(_inproj_prompt_kernel, n_heads=n_heads)
    return pl.pallas_call(
        kern,
        out_shape=(jax.ShapeDtypeStruct((3, r, wsec), BF16),
                   jax.ShapeDtypeStruct((N_SECTIONS - 3, r, wsec), F32),
                   jax.ShapeDtypeStruct((batch, wsec, seq), F32),
                   jax.ShapeDtypeStruct((r * n_heads, HEAD_DIM), F32)),
        grid=(r // tm, N_SECTIONS),
        in_specs=[pl.BlockSpec((tm, d), lambda i, s: (i, 0)),
                  pl.BlockSpec((d, wsec), lambda i, s: (0, s))],
        out_specs=(pl.BlockSpec((1, tm, wsec), lambda i, s: (jnp.minimum(s, 2), i, 0)),
                   pl.BlockSpec((1, tm, wsec), lambda i, s: (jnp.maximum(s, 3) - 3, i, 0)),
                   pl.BlockSpec((1, wsec, tm), lambda i, s: (i // tpb, 0, i % tpb)),
                   pl.BlockSpec((tm * n_heads, HEAD_DIM), lambda i, s: (i, 0))),
        compiler_params=_params("arbitrary", "arbitrary"),
        name="inproj_prompt",
    )(h, w)


def _prompt_attn_ops(par_ref, q_ref, k_ref, v_ref, g_ref, sg_ref, o_ref,
                     vt, kx_s, mask_s, sx, sy, m_s, acc_s, *, h, qt, tq, tk, out_scale):
    lam = par_ref[0]
    slope2 = par_ref[1 + h] * LOG2E
    assert tq == 2 * tk

    def fill():
        ones_row = (lax.broadcasted_iota(jnp.int32, (vt.shape[1] - HEAD_DIM, tk), 0) == 0).astype(BF16)
        for j in range(k_ref.shape[1] // tk):
            vt[j, :HEAD_DIM, :] = v_ref[0, j * tk:(j + 1) * tk, :].astype(F32).T.astype(BF16)
            vt[j, HEAD_DIM:, :] = ones_row
        jl = lax.broadcasted_iota(jnp.int32, (tk, HEAD_DIM), 0).astype(F32) * slope2
        hi = jl.astype(BF16).astype(F32)
        mid = (jl - hi).astype(BF16).astype(F32)
        lo = jl - hi - mid
        lane = lax.broadcasted_iota(jnp.int32, (tk, HEAD_DIM), 1)
        kx_s[...] = jnp.where(lane == 0, hi, jnp.where(lane == 1, mid, jnp.where(lane == 2, lo, 0.0))).astype(BF16)
        jj = lax.broadcasted_iota(jnp.int32, (tk, tq), 0)
        ii = lax.broadcasted_iota(jnp.int32, (tk, tq), 1)
        mask_s[...] = jnp.where(ii >= jj, 0.0, -jnp.inf)

    q = q_ref[0].astype(F32) * (QK_DIM ** -0.5 * LOG2E)
    lane = lax.broadcasted_iota(jnp.int32, q.shape, 1)
    qx = (lane < 3).astype(F32).astype(BF16)
    qs = (jnp.concatenate([jnp.where(lane < QK_DIM, q, 0.0).astype(BF16), qx], axis=1),
          jnp.concatenate([jnp.where(lane >= QK_DIM, q, 0.0).astype(BF16), qx], axis=1))
    c0 = -slope2 * lax.broadcasted_iota(jnp.int32, (1, tq), 1).astype(F32)

    whole = slice(0, tq)

    def qk(kt, j, cols=whole):
        keys = k_ref[0, pl.ds(pl.multiple_of(kt * tk, tk), tk), :]
        return lax.dot_general(jnp.concatenate([keys, kx_s[...]], axis=1), qs[j][cols], _NT,
                               preferred_element_type=F32)

    def update(j, s, kt, cols=whole):
        c = c0[:, cols] + slope2 * (jnp.zeros((1, s.shape[1]), jnp.int32) + (kt * tk - qt * tq)).astype(F32)
        m_old = m_s[j, :, cols]
        m_new = jnp.maximum(m_old, jnp.max(s, axis=0, keepdims=True) + c)
        alpha = jnp.exp2(m_old - m_new)
        p = jnp.exp2(s - (m_new - c))
        m_s[j, :, cols] = m_new
        acc_s[j, :, cols] = alpha * acc_s[j, :, cols] + jnp.dot(vt[kt], p.astype(BF16),
                                                                preferred_element_type=F32)

    def start():
        m_s[...] = jnp.full(m_s.shape, -jnp.inf, F32)
        acc_s[...] = jnp.zeros(acc_s.shape, F32)
        for j in range(2):
            sx[j] = qk(0, j)

    def step(u):
        ka = 2 * u
        for j in range(2):
            sy[j] = qk(ka + 1, j)
        for j in range(2):
            update(j, sx[j], ka)
        for j in range(2):
            sx[j] = qk(ka + 2, j)
        for j in range(2):
            update(j, sy[j], ka + 1)

    def finish():
        kd = 2 * qt
        late = slice(tk, tq)
        for j in range(2):
            sy[j, :, late] = qk(kd + 1, j, late)
        for j in range(2):
            update(j, sx[j] + mask_s[...], kd)
        for j in range(2):
            update(j, sy[j, :, late] + mask_s[:, :tk], kd + 1, late)
        o0 = acc_s[0]
        o1 = acc_s[1]
        od = (o0[:HEAD_DIM] / o0[HEAD_DIM:HEAD_DIM + 1] - lam * (o1[:HEAD_DIM] / o1[HEAD_DIM:HEAD_DIM + 1])).T
        y = od * lax.rsqrt(jnp.mean(od * od, axis=-1, keepdims=True) + SUBLN_EPS) * sg_ref[...] * out_scale
        o_ref[...] = (y * _silu(g_ref[0])).astype(o_ref.dtype)

    return fill, start, step, finish


def _prompt_attn_scratch(seq, tq, tk):
    return [pltpu.VMEM((seq // tk, HEAD_DIM + 16, tk), BF16),
            pltpu.VMEM((tk, HEAD_DIM), BF16),
            pltpu.VMEM((tk, tq), F32),
            pltpu.VMEM((2, tk, tq), F32),
            pltpu.VMEM((2, tk, tq), F32),
            pltpu.VMEM((2, 1, tq), F32),
            pltpu.VMEM((2, HEAD_DIM + 16, tq), F32)]


def _attn_prompt_kernel(*refs, tq, tk, out_scale):
    qt = pl.program_id(2)
    fill, start, step, finish = _prompt_attn_ops(*refs, h=pl.program_id(1), qt=qt, tq=tq, tk=tk,
                                                 out_scale=out_scale)
    pl.when(qt == 0)(fill)
    start()
    lax.fori_loop(0, qt, lambda u, c: (step(u), c)[1], 0)
    finish()


def _attn_prompt(qkv, gh, par, subln_g, batch, seq, n_heads, out_scale, tq=1024, tk=512):
    nq = seq // tq
    kern = functools.partial(_attn_prompt_kernel, tq=tq, tk=tk, out_scale=out_scale)
    return pl.pallas_call(
        kern,
        out_shape=jax.ShapeDtypeStruct((batch * seq, n_heads * HEAD_DIM), BF16),
        grid=(batch, n_heads, nq),
        in_specs=[pl.BlockSpec(memory_space=pltpu.SMEM),
                  pl.BlockSpec((1, tq, HEAD_DIM), lambda b, h, i: (0, b * nq + i, h)),
                  pl.BlockSpec((1, seq, HEAD_DIM), lambda b, h, i: (1, b, h)),
                  pl.BlockSpec((1, seq, HEAD_DIM), lambda b, h, i: (2, b, h)),
                  pl.BlockSpec((1, tq, HEAD_DIM), lambda b, h, i: (0, b * nq + i, h)),
                  pl.BlockSpec((1, HEAD_DIM), lambda b, h, i: (0, 0))],
        out_specs=pl.BlockSpec((tq, HEAD_DIM), lambda b, h, i: (b * nq + i, h)),
        scratch_shapes=_prompt_attn_scratch(seq, tq, tk),
        compiler_params=_params("arbitrary", "arbitrary", "arbitrary"),
        name="attn_prompt",
    )(par, qkv, qkv, qkv, gh, subln_g.reshape(1, HEAD_DIM))


def _attn_sample_kernel(pt_ref, par_ref, slope_ref, q_ref, kn_ref, vn_ref, g_ref, sg_ref, *rest,
                        npg, page, n_new, n_heads, out_scale):
    k_refs = rest[:npg]
    v_refs = rest[npg:2 * npg]
    o_ref = rest[2 * npg]
    qbd, b0_s, m_s, l_s, acc_s = rest[2 * npg + 1:]
    i = pl.program_id(1)
    n_steps = pl.num_programs(1)
    tblk = npg * page
    nrow, width = qbd.shape
    rows_per_head = 2 * n_new
    slope = slope_ref[...] * LOG2E

    def head_rows(hh):
        return slice(hh * rows_per_head, (hh + 1) * rows_per_head)

    @pl.when(i == 0)
    def _init():
        q = q_ref[0] * (QK_DIM ** -0.5 * LOG2E)
        qt = jnp.concatenate([q] * (nrow // n_new), axis=0)
        rr = lax.broadcasted_iota(jnp.int32, (nrow, width), 0)
        cc = lax.broadcasted_iota(jnp.int32, (nrow, width), 1)
        qbd[...] = jnp.where(rr // n_new == cc // QK_DIM, qt, 0.0).astype(BF16)
        b0_s[...] = slope * lax.broadcasted_iota(jnp.int32, (nrow, tblk), 1).astype(F32)
        zpad = jnp.zeros((nrow - n_new, width), F32)
        kn = jnp.concatenate([kn_ref[0], zpad], axis=0).astype(BF16)
        vn = jnp.concatenate([vn_ref[0], zpad], axis=0).astype(BF16)
        s = lax.dot_general(qbd[...], kn, _NT, preferred_element_type=F32)
        tk = lax.broadcasted_iota(jnp.int32, (nrow, nrow), 1)
        tqry = lax.broadcasted_iota(jnp.int32, (nrow, nrow), 0) % n_new
        s = jnp.where(tk <= tqry, s + slope * tk.astype(F32), -jnp.inf)
        m = jnp.max(s, axis=1, keepdims=True)
        p = jnp.exp2(s - m)
        m_s[...] = m
        l_s[...] = jnp.sum(p, axis=1, keepdims=True)
        pb = p.astype(BF16)
        for hh in range(n_heads):
            acc_s[head_rows(hh), :] = jnp.dot(pb[head_rows(hh), :], vn[:, hh * HEAD_DIM:(hh + 1) * HEAD_DIM],
                                              preferred_element_type=F32)

    gp = 2 if npg % 2 == 0 else 1
    qb = qbd[...]
    s = jnp.concatenate(
        [jnp.dot(qb, jnp.concatenate([k_refs[pg + t][...].astype(BF16) for t in range(gp)], axis=1),
                 preferred_element_type=F32) for pg in range(0, npg, gp)], axis=1) + b0_s[...]
    c = slope * (jnp.zeros((nrow, 1), jnp.int32) + (i - n_steps) * tblk).astype(F32)
    m_old = m_s[...]
    m_new = jnp.maximum(m_old, jnp.max(s, axis=1, keepdims=True) + c)
    alpha = jnp.exp2(m_old - m_new)
    p = jnp.exp2(s - (m_new - c))
    m_s[...] = m_new
    l_s[...] = alpha * l_s[...] + jnp.sum(p, axis=1, keepdims=True)
    pb = p.astype(BF16)
    for hh in range(n_heads):
        rs = head_rows(hh)
        pv = jnp.zeros((rows_per_head, HEAD_DIM), F32)
        for pg in range(0, npg, gp):
            vb = jnp.concatenate([v_refs[pg + t][pl.ds(hh, page, stride=n_heads), :].astype(BF16)
                                  for t in range(gp)], axis=0)
            pv = pv + jnp.dot(pb[rs, pg * page:(pg + gp) * page], vb, preferred_element_type=F32)
        acc_s[rs, :] = alpha[rs, :] * acc_s[rs, :] + pv

    @pl.when(i == n_steps - 1)
    def _finish():
        lam = par_ref[0]
        o = acc_s[...] / l_s[...]
        g = g_ref[0]
        for hh in range(n_heads):
            cs = slice(hh * HEAD_DIM, (hh + 1) * HEAD_DIM)
            blk = o[head_rows(hh), :]
            od = blk[:n_new] - lam * blk[n_new:]
            y = od * lax.rsqrt(jnp.mean(od * od, axis=-1, keepdims=True) + SUBLN_EPS) * sg_ref[...] * out_scale
            o_ref[0, :, cs] = y * _silu(g[:, cs])

def _attn_sample(z, cache_kt, cache_v, page_table, par, subln_g, n_heads, n_new, out_scale, npg):
    dec_batch, n_pages = page_table.shape
    width, page = cache_kt.shape[1], cache_kt.shape[2]
    nrow = n_heads * 2 * n_new
    assert nrow == HEAD_DIM and n_pages % npg == 0
    tblk = npg * page
    n_steps = n_pages // npg
    row_head = jnp.arange(nrow) // (2 * n_new)
    slope_col = jnp.exp2(-8.0 * (row_head + 1).astype(F32) / n_heads).reshape(nrow, 1)

    def page_spec(pg, rows, cols):
        return pl.BlockSpec((None, rows, cols), lambda s, i, pt: (pt[s, i * npg + pg], 0, 0))

    def sec_spec(sec):
        return pl.BlockSpec((1, None, n_new, width), lambda s, i, pt: (sec, s, 0, 0))

    kern = functools.partial(_attn_sample_kernel, npg=npg, page=page, n_new=n_new, n_heads=n_heads,
                             out_scale=out_scale)
    grid_spec = pltpu.PrefetchScalarGridSpec(
        num_scalar_prefetch=1,
        grid=(dec_batch, n_steps),
        in_specs=[pl.BlockSpec(memory_space=pltpu.SMEM),
                  pl.BlockSpec((nrow, 1), lambda s, i, pt: (0, 0)),
                  sec_spec(0), sec_spec(1), sec_spec(2), sec_spec(3),
                  pl.BlockSpec((1, HEAD_DIM), lambda s, i, pt: (0, 0))]
                 + [page_spec(pg, width, page) for pg in range(npg)]
                 + [page_spec(pg, page * n_heads, HEAD_DIM) for pg in range(npg)],
        out_specs=pl.BlockSpec((1, n_new, width), lambda s, i, pt: (s, 0, 0)),
        scratch_shapes=[pltpu.VMEM((nrow, width), BF16),
                        pltpu.VMEM((nrow, tblk), F32),
                        pltpu.VMEM((nrow, 1), F32),
                        pltpu.VMEM((nrow, 1), F32),
                        pltpu.VMEM((nrow, HEAD_DIM), F32)],
    )
    return pl.pallas_call(
        kern,
        out_shape=jax.ShapeDtypeStruct((dec_batch, n_new, width), F32),
        grid_spec=grid_spec,
        compiler_params=_params("arbitrary", "arbitrary"),
        name="attn_sample",
    )(page_table, par, slope_col, z, z, z, z, subln_g.reshape(1, HEAD_DIM),
      *([cache_kt] * npg), *([cache_v] * npg))


def _hgrn_kernel(*refs, chunk, n_heads, has_init, chunk_is_sequence):
    if has_init:
        (q_ref, f_ref, v_ref, g_ref, lb_ref, ng_ref, s0_ref, o_ref, sout_ref,
         st, b_s, k_s, qe_s, ke_s, o_s, u_s, stc_s) = refs
    else:
        (q_ref, f_ref, v_ref, g_ref, lb_ref, ng_ref, o_ref, sout_ref,
         st, b_s, k_s, qe_s, ke_s, o_s, u_s, stc_s) = refs
        s0_ref = None
    i = pl.program_id(2)
    tb, width = f_ref.shape[1], f_ref.shape[2]
    nchunk = tb // chunk
    grp = min(tb, HG_GROUP)
    levels = [w for w in (32, 16, 8) if 2 * w <= chunk]
    assert chunk % HG_BLOCK == 0 and grp % chunk == 0 and tb % grp == 0
    heads = [slice(hh * HEAD_DIM, (hh + 1) * HEAD_DIM) for hh in range(n_heads)]

    if not chunk_is_sequence:
        @pl.when(i == 0)
        def _init():
            for hh in range(n_heads):
                st[hh] = s0_ref[0, hh].T if has_init else jnp.zeros((HEAD_DIM, HEAD_DIM), F32)

    lb = lb_ref[...]
    forget = lb + (1.0 - lb) * jax.nn.sigmoid(f_ref[0])
    kk = 1.0 - forget
    b = jnp.log(forget) * LOG2E
    row = lax.broadcasted_iota(jnp.int32, b.shape, 0) % chunk
    sh = 1
    while sh < chunk:
        b = b + jnp.where(row >= sh, pltpu.roll(b, sh, axis=0), 0.0)
        sh *= 2
    b3 = b.reshape(nchunk, chunk, width)
    b_end = jnp.broadcast_to(b3[:, chunk - 1:chunk, :], b3.shape).reshape(tb, width)
    b_s[...] = b
    k_s[...] = kk
    qe_s[...] = q_ref[0] * jnp.exp2(b)
    ke_s[...] = kk * jnp.exp2(b_end - b)

    rowi = lax.broadcasted_iota(jnp.int32, (grp, HEAD_DIM), 0)
    ti = lax.broadcasted_iota(jnp.int32, (grp, grp), 0)
    si = lax.broadcasted_iota(jnp.int32, (grp, grp), 1)

    def block_row(x, period, r):
        x3 = x.reshape(grp // period, period, HEAD_DIM)
        return jnp.broadcast_to(x3[:, r:r + 1, :], x3.shape).reshape(grp, HEAD_DIM)

    def neg_unless(cond):
        return jnp.where(cond, 0.0, -jnp.inf)

    same_block = ti // HG_BLOCK == si // HG_BLOCK
    row_from = [neg_unless(rowi % HG_BLOCK >= s) for s in range(HG_BLOCK)]
    pair_at = [(same_block & (si % HG_BLOCK == s)).astype(F32) for s in range(HG_BLOCK)]
    split = [(neg_unless(rowi % (2 * w) >= w), neg_unless(rowi % (2 * w) < w),
              (ti // (2 * w) == si // (2 * w)).astype(F32)) for w in levels]

    hp = 2 if n_heads % 2 == 0 else 1
    packs = [heads[i:i + hp] for i in range(0, n_heads, hp)]

    def side(mats):
        return mats[0] if len(mats) == 1 else jnp.concatenate(mats, axis=1)

    def block_diag(mats):
        if len(mats) == 1:
            return mats[0]
        a, b = mats
        return jnp.concatenate([jnp.concatenate([a, jnp.zeros_like(a)], axis=1),
                                jnp.concatenate([jnp.zeros_like(b), b], axis=1)], axis=0)

    pair_at = [side([m] * hp) for m in pair_at]
    split = [(later, earlier, side([same] * hp)) for later, earlier, same in split]

    def group_body(gi, carry):
        rs = pl.ds(pl.multiple_of(gi * grp, grp), grp)
        for pack in packs:
            qg = [q_ref[0, rs, cs] for cs in pack]
            kg = [k_s[rs, cs] for cs in pack]
            bg = [b_s[rs, cs] for cs in pack]
            kbd = block_diag([k.astype(BF16) for k in kg])
            a = jnp.zeros((grp, hp * grp), F32)
            for s in range(HG_BLOCK):
                y = [(q * jnp.exp2(b - block_row(b, HG_BLOCK, s) + row_from[s])).astype(BF16)
                     for q, b in zip(qg, bg)]
                a = a + lax.dot_general(side(y), kbd, _NT, preferred_element_type=F32) * pair_at[s]
            for w, (later, earlier, same_parent) in zip(levels, split):
                d = [b - block_row(b, 2 * w, w) for b in bg]
                qw = [(q * jnp.exp2(dd + later)).astype(BF16) for q, dd in zip(qg, d)]
                kw = [(k * jnp.exp2(earlier - dd)).astype(BF16) for k, dd in zip(kg, d)]
                a = a + lax.dot_general(side(qw), block_diag(kw), _NT,
                                        preferred_element_type=F32) * same_parent
            vbd = block_diag([v_ref[0, rs, cs].astype(BF16) for cs in pack])
            o_s[rs, pack[0].start:pack[-1].stop] = jnp.dot(a.astype(BF16), vbd, preferred_element_type=F32)
        return carry

    lax.fori_loop(0, tb // grp, group_body, 0, unroll=math.gcd(tb // grp, 4))

    for hh, cs in enumerate(heads):
        for c in range(nchunk):
            rs = slice(c * chunk, (c + 1) * chunk)
            u_s[c] = lax.dot_general(v_ref[0, rs, cs].astype(BF16), ke_s[rs, cs].astype(BF16), _TN,
                                     preferred_element_type=F32)
        s_cur = None if chunk_is_sequence else st[hh]
        for c in range(nchunk):
            if chunk_is_sequence:
                s_cur = s0_ref[c, hh].T
            stc_s[c] = s_cur.astype(BF16)
            last = (c + 1) * chunk - 1
            s_cur = s_cur * jnp.exp2(b_s[last:last + 1, cs]) + u_s[c]
            if chunk_is_sequence:
                sout_ref[c, hh] = s_cur.T
        if not chunk_is_sequence:
            st[hh] = s_cur
        for c in range(nchunk):
            rs = slice(c * chunk, (c + 1) * chunk)
            o_s[rs, cs] += lax.dot_general(qe_s[rs, cs].astype(BF16), stc_s[c], _NT,
                                           preferred_element_type=F32)

    for cs in heads:
        o = o_s[:, cs]
        y = o * lax.rsqrt(jnp.mean(o * o, axis=-1, keepdims=True) + NORM_EPS) * ng_ref[...]
        o_ref[0, :, cs] = (y * _silu(g_ref[0, :, cs])).astype(o_ref.dtype)

    if not chunk_is_sequence:
        @pl.when(i == pl.num_programs(2) - 1)
        def _emit():
            for hh in range(n_heads):
                sout_ref[0, hh] = st[hh].T


def _hgrn(z, sec0, lb, norm_g, s0, batch, length, n_heads, heads_per_step, tb, chunk, out_dtype, seqs_per_step=1):
    width = z.shape[2]
    wblk = heads_per_step * HEAD_DIM
    has_init = s0 is not None
    chunk_is_sequence = seqs_per_step > 1
    if chunk_is_sequence:
        assert has_init and tb == chunk == length and batch % seqs_per_step == 0
        batch, length, tb = batch // seqs_per_step, length * seqs_per_step, tb * seqs_per_step
    nblk = length // tb

    def sec_spec(sec):
        return pl.BlockSpec((1, tb, wblk), lambda b, h, i: (sec, b * nblk + i, h))

    in_specs = [sec_spec(sec0), sec_spec(sec0 + 1), sec_spec(sec0 + 2), sec_spec(sec0 + 3),
                pl.BlockSpec((1, wblk), lambda b, h, i: (0, h)),
                pl.BlockSpec((1, HEAD_DIM), lambda b, h, i: (0, 0))]
    args = [z, z, z, z, lb.reshape(1, width), norm_g.reshape(1, HEAD_DIM)]
    state_spec = pl.BlockSpec((seqs_per_step, heads_per_step, HEAD_DIM, HEAD_DIM), lambda b, h, i: (b, h, 0, 0))
    if has_init:
        in_specs.append(state_spec)
        args.append(s0)
    kern = functools.partial(_hgrn_kernel, chunk=chunk, n_heads=heads_per_step, has_init=has_init,
                             chunk_is_sequence=chunk_is_sequence)
    return pl.pallas_call(
        kern,
        out_shape=(jax.ShapeDtypeStruct((1, batch * length, width), out_dtype),
                   jax.ShapeDtypeStruct((batch * seqs_per_step, n_heads, HEAD_DIM, HEAD_DIM), F32)),
        grid=(batch, n_heads // heads_per_step, nblk),
        in_specs=in_specs,
        out_specs=(pl.BlockSpec((1, tb, wblk), lambda b, h, i: (0, b * nblk + i, h)), state_spec),
        scratch_shapes=[pltpu.VMEM((heads_per_step, HEAD_DIM, HEAD_DIM), F32)]
                       + [pltpu.VMEM((tb, wblk), F32)] * 5
                       + [pltpu.VMEM((tb // chunk, HEAD_DIM, HEAD_DIM), F32),
                          pltpu.VMEM((tb // chunk, HEAD_DIM, HEAD_DIM), BF16)],
        compiler_params=_params("arbitrary", "arbitrary", "arbitrary"),
        name="hgrn",
    )(*args)


def _outproj_kernel(ma_ref, mh_ref, w_ref, x_ref, rg_ref, fg_ref, y_ref, *, final_norm):
    half = ma_ref.shape[1]
    u = jnp.dot(ma_ref[...].astype(BF16), w_ref[:half, :], preferred_element_type=F32)
    u = u + jnp.dot(mh_ref[...].astype(BF16), w_ref[half:, :], preferred_element_type=F32)
    r = x_ref[...] + rg_ref[0] * u
    if final_norm:
        r = r * lax.rsqrt(jnp.mean(r * r, axis=-1, keepdims=True) + NORM_EPS) * fg_ref[...]
    y_ref[...] = r


def _outproj(m_att, m_hg, w_bf, x, rg, final_g, tm, rows_per_mod, final_norm):
    r, d = x.shape
    half = m_att.shape[1]
    rb = rg.shape[1]
    tiles_per_mod = rows_per_mod // tm
    return pl.pallas_call(
        functools.partial(_outproj_kernel, final_norm=final_norm),
        out_shape=jax.ShapeDtypeStruct((r, d), F32),
        grid=(r // tm,),
        in_specs=[pl.BlockSpec((tm, half), lambda i: (i, 0)),
                  pl.BlockSpec((tm, half), lambda i: (i, 0)),
                  pl.BlockSpec((2 * half, d), lambda i: (0, 0)),
                  pl.BlockSpec((tm, d), lambda i: (i, 0)),
                  pl.BlockSpec((1, rb, d), lambda i: (i // tiles_per_mod, 0, 0)),
                  pl.BlockSpec((1, d), lambda i: (0, 0))],
        out_specs=pl.BlockSpec((tm, d), lambda i: (i, 0)),
        compiler_params=_params("arbitrary"),
        name="outproj",
    )(m_att, m_hg, w_bf, x, rg, final_g.reshape(1, d))


def _tile_plan(seq, n_pages):
    return {"rows": math.gcd(seq, 1024), "out_rows": math.gcd(seq, 512),
            "pages": math.gcd(n_pages, SAMPLE_PAGES_PER_STEP)}


def kernel(x_prompt, x_sample, cache_k, cache_v, state_hgrn, page_table, c_prompt, c_sample,
           norm_g, w_ada, b_ada, w_in, lambda_q1, lambda_k1, lambda_q2, lambda_k2, subln_g,
           hg_lower_bounds, hg_norm_g, w_out, final_g):
    batch, seq, d = x_prompt.shape
    dec_batch, n_new, _ = x_sample.shape
    depth, n_pool, page, n_heads, _, qk_dim = cache_k.shape
    hg_heads = state_hgrn.shape[2]
    width = d // 2
    assert qk_dim == QK_DIM and n_heads * HEAD_DIM == width and hg_heads * HEAD_DIM == width
    assert w_in.shape[2] == N_SECTIONS * width
    tiles = _tile_plan(seq, page_table.shape[1])

    slopes = jnp.asarray([2.0 ** (-8.0 * (h + 1) / n_heads) for h in range(n_heads)], F32)
    lb_all = jnp.cumsum(jax.nn.softmax(hg_lower_bounds.astype(F32), axis=0), axis=0)
    n_mod = batch + dec_batch
    n_mod_pad = -(-n_mod // 8) * 8
    c_all = jnp.concatenate([c_prompt, c_sample, jnp.zeros((n_mod_pad - n_mod, d), F32)], axis=0)

    xp = x_prompt.reshape(batch * seq, d)
    xs = x_sample.reshape(dec_batch * n_new, d)
    rs = dec_batch * n_new
    outs = [[] for _ in range(6)]
    for l in range(depth):
        last = l == depth - 1
        lam_init = 0.8 - 0.6 * math.exp(-0.3 * l)
        lam = (jnp.exp(jnp.sum(lambda_q1[l].astype(F32) * lambda_k1[l].astype(F32)))
               - jnp.exp(jnp.sum(lambda_q2[l].astype(F32) * lambda_k2[l].astype(F32))) + lam_init)
        par = jnp.concatenate([lam.reshape(1), slopes, jnp.zeros((7,), F32)])
        w_out_bf = w_out[l].astype(BF16)

        mod = _adaln(c_all, w_ada[l], b_ada[l])
        shift, scale, rgate = mod[:, :d], mod[:, d:2 * d], mod[:, 2 * d:]

        def rows_p(t):
            return t[:batch].reshape(batch, 1, d)

        def rows_s(t):
            return jnp.repeat(t[batch:n_mod], n_new, axis=0).reshape(1, rs, d)

        t = tiles
        hs = _modnorm(xs, norm_g[l], rows_s(scale), rows_s(shift), tm=rs, rows_per_mod=rs)
        zs, w_in_bf = _inproj(hs, w_in[l])
        hp = _modnorm(xp, norm_g[l], rows_p(scale), rows_p(shift), tm=t["rows"], rows_per_mod=seq)
        qkv, gh, k_t, v_rows = _inproj_prompt(hp, w_in_bf, batch, seq, n_heads, tm=t["rows"])
        cache_kt = jnp.transpose(cache_k[l], (0, 2, 3, 4, 1)).reshape(n_pool, width, page)
        cache_vr = cache_v[l].reshape(n_pool, page * n_heads, HEAD_DIM)
        att_s = _attn_sample(zs.reshape(N_SECTIONS, dec_batch, n_new, width), cache_kt, cache_vr,
                             page_table, par, subln_g[l], n_heads, n_new, 1.0 - lam_init, npg=t["pages"])
        att_p = _attn_prompt(qkv, gh, par, subln_g[l], batch, seq, n_heads, 1.0 - lam_init,
                             tq=t["rows"], tk=t["rows"] // 2)
        hg_p, s_p = _hgrn(gh, 1, lb_all[l], hg_norm_g[l], None, batch, seq, hg_heads,
                          heads_per_step=2 if hg_heads % 2 == 0 else 1,
                          tb=t["rows"], chunk=math.gcd(seq, HG_CHUNK), out_dtype=BF16)
        xp = _outproj(att_p, hg_p[0], w_out_bf, xp, rows_p(rgate), final_g,
                      tm=t["out_rows"], rows_per_mod=seq, final_norm=last)
        k_p = jnp.transpose(k_t.reshape(batch, n_heads, 2, QK_DIM, seq), (0, 4, 1, 2, 3))
        outs[0].append(k_p)
        outs[1].append(v_rows.reshape(batch, seq, n_heads, HEAD_DIM))
        outs[2].append(s_p)

        hg_s, s_s = _hgrn(zs, 4, lb_all[l], hg_norm_g[l], state_hgrn[l], dec_batch, n_new, hg_heads,
                          heads_per_step=hg_heads, tb=n_new, chunk=n_new, out_dtype=F32,
                          seqs_per_step=math.gcd(dec_batch, 8))
        xs = _outproj(att_s.reshape(rs, width), hg_s[0], w_out_bf, xs, rows_s(rgate), final_g,
                      tm=rs, rows_per_mod=rs, final_norm=last)
        outs[3].append(zs[1].reshape(dec_batch, n_new, n_heads, 2, QK_DIM))
        outs[4].append(zs[2].reshape(dec_batch, n_new, n_heads, HEAD_DIM))
        outs[5].append(s_s)

    y_prompt = xp.reshape(batch, seq, d)
    y_sample = xs.reshape(dec_batch, n_new, d)
    return (y_prompt, y_sample) + tuple(jnp.stack(o) for o in outs)
```

```python
import functools
import math

import jax
import jax.numpy as jnp
from jax import lax
from jax.experimental import pallas as pl
from jax.experimental.pallas import tpu as pltpu

F32 = jnp.float32
BF16 = jnp.bfloat16

NORM_EPS = 1e-6
SUBLN_EPS = 1e-5
N_SECTIONS = 8
HEAD_DIM = 128
QK_DIM = 64
HG_CHUNK = 64
HG_BLOCK = 8
HG_GROUP = 128
LOG2E = 1.4426950408889634
SAMPLE_PAGES_PER_STEP = 16
VMEM_LIMIT_BYTES = 56 * 1024 * 1024

_NT = (((1,), (1,)), ((), ()))
_TN = (((0,), (0,)), ((), ()))


def _params(*sem):
    return pltpu.CompilerParams(dimension_semantics=sem, vmem_limit_bytes=VMEM_LIMIT_BYTES)


def _silu(x):
    return x * jax.nn.sigmoid(x)


def _adaln_kernel(c_ref, w_ref, b_ref, o_ref):
    a = _silu(c_ref[...]).astype(BF16)
    o_ref[...] = jnp.dot(a, w_ref[...].astype(BF16), preferred_element_type=F32) + b_ref[...]


def _adaln(c, w, b, tn=1024):
    m, d = c.shape
    n = w.shape[1]
    return pl.pallas_call(
        _adaln_kernel,
        out_shape=jax.ShapeDtypeStruct((m, n), F32),
        grid=(n // tn,),
        in_specs=[pl.BlockSpec((m, d), lambda j: (0, 0)),
                  pl.BlockSpec((d, tn), lambda j: (0, j)),
                  pl.BlockSpec((1, tn), lambda j: (0, j))],
        out_specs=pl.BlockSpec((m, tn), lambda j: (0, j)),
        compiler_params=_params("arbitrary"),
        name="adaln",
    )(c, w, b.reshape(1, n))


def _modnorm_kernel(x_ref, g_ref, sc_ref, sh_ref, h_ref):
    x = x_ref[...]
    y = x * lax.rsqrt(jnp.mean(x * x, axis=-1, keepdims=True) + NORM_EPS) * g_ref[...]
    h_ref[...] = (y * (1.0 + sc_ref[0]) + sh_ref[0]).astype(h_ref.dtype)


def _modnorm(x, g, scale, shift, tm, rows_per_mod):
    r, d = x.shape
    rb = scale.shape[1]
    tiles_per_mod = rows_per_mod // tm
    mod_spec = pl.BlockSpec((1, rb, d), lambda i: (i // tiles_per_mod, 0, 0))
    return pl.pallas_call(
        _modnorm_kernel,
        out_shape=jax.ShapeDtypeStruct((r, d), BF16),
        grid=(r // tm,),
        in_specs=[pl.BlockSpec((tm, d), lambda i: (i, 0)),
                  pl.BlockSpec((1, d), lambda i: (0, 0)),
                  mod_spec, mod_spec],
        out_specs=pl.BlockSpec((tm, d), lambda i: (i, 0)),
        compiler_params=_params("arbitrary"),
        name="modnorm",
    )(x, g.reshape(1, d), scale, shift)


def _inproj_kernel(h_ref, w_ref, o_ref, wb_ref):
    wb = w_ref[...].astype(BF16)
    wb_ref[...] = wb
    o_ref[0] = jnp.dot(h_ref[...], wb, preferred_element_type=F32)


def _inproj(h, w):
    r, d = h.shape
    wsec = w.shape[1] // N_SECTIONS
    return pl.pallas_call(
        _inproj_kernel,
        out_shape=(jax.ShapeDtypeStruct((N_SECTIONS, r, wsec), F32),
                   jax.ShapeDtypeStruct(w.shape, BF16)),
        grid=(N_SECTIONS,),
        in_specs=[pl.BlockSpec((r, d), lambda s: (0, 0)),
                  pl.BlockSpec((d, wsec), lambda s: (0, s))],
        out_specs=(pl.BlockSpec((1, r, wsec), lambda s: (s, 0, 0)),
                   pl.BlockSpec((d, wsec), lambda s: (0, s))),
        compiler_params=_params("arbitrary"),
        name="inproj",
    )(h, w)


def _inproj_prompt_kernel(h_ref, w_ref, qkv_ref, gh_ref, kt_ref, vo_ref, *, n_heads):
    s = pl.program_id(1)
    tm = h_ref.shape[0]
    z = jnp.dot(h_ref[...], w_ref[...], preferred_element_type=F32)

    @pl.when(s <= 2)
    def _():
        qkv_ref[0] = z.astype(BF16)

    @pl.when(s == 1)
    def _():
        kt_ref[0] = z.T

    @pl.when(s == 2)
    def _():
        for hh in range(n_heads):
            vo_ref[pl.ds(hh, tm, stride=n_heads), :] = z[:, hh * HEAD_DIM:(hh + 1) * HEAD_DIM]

    @pl.when(s >= 3)
    def _():
        gh_ref[0] = z


def _inproj_prompt(h, w, batch, seq, n_heads, tm):
    r, d = h.shape
    wsec = w.shape[1] // N_SECTIONS
    tpb = seq // tm
    kern = functools.partial(_inproj_prompt_kernel, n_heads=n_heads)
    return pl.pallas_call(
        kern,
        out_shape=(jax.ShapeDtypeStruct((3, r, wsec), BF16),
                   jax.ShapeDtypeStruct((N_SECTIONS - 3, r, wsec), F32),
                   jax.ShapeDtypeStruct((batch, wsec, seq), F32),
                   jax.ShapeDtypeStruct((r * n_heads, HEAD_DIM), F32)),
        grid=(r // tm, N_SECTIONS),
        in_specs=[pl.BlockSpec((tm, d), lambda i, s: (i, 0)),
                  pl.BlockSpec((d, wsec), lambda i, s: (0, s))],
        out_specs=(pl.BlockSpec((1, tm, wsec), lambda i, s: (jnp.minimum(s, 2), i, 0)),
                   pl.BlockSpec((1, tm, wsec), lambda i, s: (jnp.maximum(s, 3) - 3, i, 0)),
                   pl.BlockSpec((1, wsec, tm), lambda i, s: (i // tpb, 0, i % tpb)),
                   pl.BlockSpec((tm * n_heads, HEAD_DIM), lambda i, s: (i, 0))),
        compiler_params=_params("arbitrary", "arbitrary"),
        name="inproj_prompt",
    )(h, w)


def _prompt_attn_ops(par_ref, q_ref, k_ref, v_ref, g_ref, sg_ref, o_ref,
                     vt, kx_s, mask_s, sx, sy, m_s, acc_s, *, h, qt, tq, tk, out_scale):
    lam = par_ref[0]
    slope2 = par_ref[1 + h] * LOG2E
    assert tq == 2 * tk

    def fill():
        ones_row = (lax.broadcasted_iota(jnp.int32, (vt.shape[1] - HEAD_DIM, tk), 0) == 0).astype(BF16)
        for j in range(k_ref.shape[1] // tk):
            vt[j, :HEAD_DIM, :] = v_ref[0, j * tk:(j + 1) * tk, :].astype(F32).T.astype(BF16)
            vt[j, HEAD_DIM:, :] = ones_row
        jl = lax.broadcasted_iota(jnp.int32, (tk, HEAD_DIM), 0).astype(F32) * slope2
        hi = jl.astype(BF16).astype(F32)
        mid = (jl - hi).astype(BF16).astype(F32)
        lo = jl - hi - mid
        lane = lax.broadcasted_iota(jnp.int32, (tk, HEAD_DIM), 1)
        kx_s[...] = jnp.where(lane == 0, hi, jnp.where(lane == 1, mid, jnp.where(lane == 2, lo, 0.0))).astype(BF16)
        jj = lax.broadcasted_iota(jnp.int32, (tk, tq), 0)
        ii = lax.broadcasted_iota(jnp.int32, (tk, tq), 1)
        mask_s[...] = jnp.where(ii >= jj, 0.0, -jnp.inf)

    q = q_ref[0].astype(F32) * (QK_DIM ** -0.5 * LOG2E)
    lane = lax.broadcasted_iota(jnp.int32, q.shape, 1)
    qx = (lane < 3).astype(F32).astype(BF16)
    qs = (jnp.concatenate([jnp.where(lane < QK_DIM, q, 0.0).astype(BF16), qx], axis=1),
          jnp.concatenate([jnp.where(lane >= QK_DIM, q, 0.0).astype(BF16), qx], axis=1))
    c0 = -slope2 * lax.broadcasted_iota(jnp.int32, (1, tq), 1).astype(F32)

    whole = slice(0, tq)

    def qk(kt, j, cols=whole):
        keys = k_ref[0, pl.ds(pl.multiple_of(kt * tk, tk), tk), :]
        return lax.dot_general(jnp.concatenate([keys, kx_s[...]], axis=1), qs[j][cols], _NT,
                               preferred_element_type=F32)

    def update(j, s, kt, cols=whole):
        c = c0[:, cols] + slope2 * (jnp.zeros((1, s.shape[1]), jnp.int32) + (kt * tk - qt * tq)).astype(F32)
        m_old = m_s[j, :, cols]
        m_new = jnp.maximum(m_old, jnp.max(s, axis=0, keepdims=True) + c)
        alpha = jnp.exp2(m_old - m_new)
        p = jnp.exp2(s - (m_new - c))
        m_s[j, :, cols] = m_new
        acc_s[j, :, cols] = alpha * acc_s[j, :, cols] + jnp.dot(vt[kt], p.astype(BF16),
                                                                preferred_element_type=F32)

    def start():
        m_s[...] = jnp.full(m_s.shape, -jnp.inf, F32)
        acc_s[...] = jnp.zeros(acc_s.shape, F32)
        for j in range(2):
            sx[j] = qk(0, j)

    def step(u):
        ka = 2 * u
        for j in range(2):
            sy[j] = qk(ka + 1, j)
        for j in range(2):
            update(j, sx[j], ka)
        for j in range(2):
            sx[j] = qk(ka + 2, j)
        for j in range(2):
            update(j, sy[j], ka + 1)

    def finish():
        kd = 2 * qt
        late = slice(tk, tq)
        for j in range(2):
            sy[j, :, late] = qk(kd + 1, j, late)
        for j in range(2):
            update(j, sx[j] + mask_s[...], kd)
        for j in range(2):
            update(j, sy[j, :, late] + mask_s[:, :tk], kd + 1, late)
        o0 = acc_s[0]
        o1 = acc_s[1]
        od = (o0[:HEAD_DIM] / o0[HEAD_DIM:HEAD_DIM + 1] - lam * (o1[:HEAD_DIM] / o1[HEAD_DIM:HEAD_DIM + 1])).T
        y = od * lax.rsqrt(jnp.mean(od * od, axis=-1, keepdims=True) + SUBLN_EPS) * sg_ref[...] * out_scale
        o_ref[...] = (y * _silu(g_ref[0])).astype(o_ref.dtype)

    return fill, start, step, finish


def _prompt_attn_scratch(seq, tq, tk):
    return [pltpu.VMEM((seq // tk, HEAD_DIM + 16, tk), BF16),
            pltpu.VMEM((tk, HEAD_DIM), BF16),
            pltpu.VMEM((tk, tq), F32),
            pltpu.VMEM((2, tk, tq), F32),
            pltpu.VMEM((2, tk, tq), F32),
            pltpu.VMEM((2, 1, tq), F32),
            pltpu.VMEM((2, HEAD_DIM + 16, tq), F32)]


def _attn_prompt_kernel(*refs, tq, tk, out_scale):
    qt = pl.program_id(2)
    fill, start, step, finish = _prompt_attn_ops(*refs, h=pl.program_id(1), qt=qt, tq=tq, tk=tk,
                                                 out_scale=out_scale)
    pl.when(qt == 0)(fill)
    start()
    lax.fori_loop(0, qt, lambda u, c: (step(u), c)[1], 0)
    finish()


def _attn_prompt(qkv, gh, par, subln_g, batch, seq, n_heads, out_scale, tq=1024, tk=512):
    nq = seq // tq
    kern = functools.partial(_attn_prompt_kernel, tq=tq, tk=tk, out_scale=out_scale)
    return pl.pallas_call(
        kern,
        out_shape=jax.ShapeDtypeStruct((batch * seq, n_heads * HEAD_DIM), BF16),
        grid=(batch, n_heads, nq),
        in_specs=[pl.BlockSpec(memory_space=pltpu.SMEM),
                  pl.BlockSpec((1, tq, HEAD_DIM), lambda b, h, i: (0, b * nq + i, h)),
                  pl.BlockSpec((1, seq, HEAD_DIM), lambda b, h, i: (1, b, h)),
                  pl.BlockSpec((1, seq, HEAD_DIM), lambda b, h, i: (2, b, h)),
                  pl.BlockSpec((1, tq, HEAD_DIM), lambda b, h, i: (0, b * nq + i, h)),
                  pl.BlockSpec((1, HEAD_DIM), lambda b, h, i: (0, 0))],
        out_specs=pl.BlockSpec((tq, HEAD_DIM), lambda b, h, i: (b * nq + i, h)),
        scratch_shapes=_prompt_attn_scratch(seq, tq, tk),
        compiler_params=_params("arbitrary", "arbitrary", "arbitrary"),
        name="attn_prompt",
    )(par, qkv, qkv, qkv, gh, subln_g.reshape(1, HEAD_DIM))


def _attn_sample_kernel(pt_ref, par_ref, slope_ref, q_ref, kn_ref, vn_ref, g_ref, sg_ref, *rest,
                        npg, page, n_new, n_heads, out_scale):
    k_refs = rest[:npg]
    v_refs = rest[npg:2 * npg]
    o_ref = rest[2 * npg]
    qbd, b0_s, m_s, l_s, acc_s = rest[2 * npg + 1:]
    i = pl.program_id(1)
    n_steps = pl.num_programs(1)
    tblk = npg * page
    nrow, width = qbd.shape
    rows_per_head = 2 * n_new
    slope = slope_ref[...] * LOG2E

    def head_rows(hh):
        return slice(hh * rows_per_head, (hh + 1) * rows_per_head)

    @pl.when(i == 0)
    def _init():
        q = q_ref[0] * (QK_DIM ** -0.5 * LOG2E)
        qt = jnp.concatenate([q] * (nrow // n_new), axis=0)
        rr = lax.broadcasted_iota(jnp.int32, (nrow, width), 0)
        cc = lax.broadcasted_iota(jnp.int32, (nrow, width), 1)
        qbd[...] = jnp.where(rr // n_new == cc // QK_DIM, qt, 0.0).astype(BF16)
        b0_s[...] = slope * lax.broadcasted_iota(jnp.int32, (nrow, tblk), 1).astype(F32)
        zpad = jnp.zeros((nrow - n_new, width), F32)
        kn = jnp.concatenate([kn_ref[0], zpad], axis=0).astype(BF16)
        vn = jnp.concatenate([vn_ref[0], zpad], axis=0).astype(BF16)
        s = lax.dot_general(qbd[...], kn, _NT, preferred_element_type=F32)
        tk = lax.broadcasted_iota(jnp.int32, (nrow, nrow), 1)
        tqry = lax.broadcasted_iota(jnp.int32, (nrow, nrow), 0) % n_new
        s = jnp.where(tk <= tqry, s + slope * tk.astype(F32), -jnp.inf)
        m = jnp.max(s, axis=1, keepdims=True)
        p = jnp.exp2(s - m)
        m_s[...] = m
        l_s[...] = jnp.sum(p, axis=1, keepdims=True)
        pb = p.astype(BF16)
        for hh in range(n_heads):
            acc_s[head_rows(hh), :] = jnp.dot(pb[head_rows(hh), :], vn[:, hh * HEAD_DIM:(hh + 1) * HEAD_DIM],
                                              preferred_element_type=F32)

    gp = 2 if npg % 2 == 0 else 1
    qb = qbd[...]
    s = jnp.concatenate(
        [jnp.dot(qb, jnp.concatenate([k_refs[pg + t][...].astype(BF16) for t in range(gp)], axis=1),
                 preferred_element_type=F32) for pg in range(0, npg, gp)], axis=1) + b0_s[...]
    c = slope * (jnp.zeros((nrow, 1), jnp.int32) + (i - n_steps) * tblk).astype(F32)
    m_old = m_s[...]
    m_new = jnp.maximum(m_old, jnp.max(s, axis=1, keepdims=True) + c)
    alpha = jnp.exp2(m_old - m_new)
    p = jnp.exp2(s - (m_new - c))
    m_s[...] = m_new
    l_s[...] = alpha * l_s[...] + jnp.sum(p, axis=1, keepdims=True)
    pb = p.astype(BF16)
    for hh in range(n_heads):
        rs = head_rows(hh)
        pv = jnp.zeros((rows_per_head, HEAD_DIM), F32)
        for pg in range(0, npg, gp):
            vb = jnp.concatenate([v_refs[pg + t][pl.ds(hh, page, stride=n_heads), :].astype(BF16)
                                  for t in range(gp)], axis=0)
            pv = pv + jnp.dot(pb[rs, pg * page:(pg + gp) * page], vb, preferred_element_type=F32)
        acc_s[rs, :] = alpha[rs, :] * acc_s[rs, :] + pv

    @pl.when(i == n_steps - 1)
    def _finish():
        lam = par_ref[0]
        o = acc_s[...] / l_s[...]
        g = g_ref[0]
        for hh in range(n_heads):
            cs = slice(hh * HEAD_DIM, (hh + 1) * HEAD_DIM)
            blk = o[head_rows(hh), :]
            od = blk[:n_new] - lam * blk[n_new:]
            y = od * lax.rsqrt(jnp.mean(od * od, axis=-1, keepdims=True) + SUBLN_EPS) * sg_ref[...] * out_scale
            o_ref[0, :, cs] = y * _silu(g[:, cs])

def _attn_sample(z, cache_kt, cache_v, page_table, par, subln_g, n_heads, n_new, out_scale, npg):
    dec_batch, n_pages = page_table.shape
    width, page = cache_kt.shape[1], cache_kt.shape[2]
    nrow = n_heads * 2 * n_new
    assert nrow == HEAD_DIM and n_pages % npg == 0
    tblk = npg * page
    n_steps = n_pages // npg
    row_head = jnp.arange(nrow) // (2 * n_new)
    slope_col = jnp.exp2(-8.0 * (row_head + 1).astype(F32) / n_heads).reshape(nrow, 1)

    def page_spec(pg, rows, cols):
        return pl.BlockSpec((None, rows, cols), lambda s, i, pt: (pt[s, i * npg + pg], 0, 0))

    def sec_spec(sec):
        return pl.BlockSpec((1, None, n_new, width), lambda s, i, pt: (sec, s, 0, 0))

    kern = functools.partial(_attn_sample_kernel, npg=npg, page=page, n_new=n_new, n_heads=n_heads,
                             out_scale=out_scale)
    grid_spec = pltpu.PrefetchScalarGridSpec(
        num_scalar_prefetch=1,
        grid=(dec_batch, n_steps),
        in_specs=[pl.BlockSpec(memory_space=pltpu.SMEM),
                  pl.BlockSpec((nrow, 1), lambda s, i, pt: (0, 0)),
                  sec_spec(0), sec_spec(1), sec_spec(2), sec_spec(3),
                  pl.BlockSpec((1, HEAD_DIM), lambda s, i, pt: (0, 0))]
                 + [page_spec(pg, width, page) for pg in range(npg)]
                 + [page_spec(pg, page * n_heads, HEAD_DIM) for pg in range(npg)],
        out_specs=pl.BlockSpec((1, n_new, width), lambda s, i, pt: (s, 0, 0)),
        scratch_shapes=[pltpu.VMEM((nrow, width), BF16),
                        pltpu.VMEM((nrow, tblk), F32),
                        pltpu.VMEM((nrow, 1), F32),
                        pltpu.VMEM((nrow, 1), F32),
                        pltpu.VMEM((nrow, HEAD_DIM), F32)],
    )
    return pl.pallas_call(
        kern,
        out_shape=jax.ShapeDtypeStruct((dec_batch, n_new, width), F32),
        grid_spec=grid_spec,
        compiler_params=_params("arbitrary", "arbitrary"),
        name="attn_sample",
    )(page_table, par, slope_col, z, z, z, z, subln_g.reshape(1, HEAD_DIM),
      *([cache_kt] * npg), *([cache_v] * npg))


def _hgrn_kernel(*refs, chunk, n_heads, has_init, chunk_is_sequence):
    if has_init:
        (q_ref, f_ref, v_ref, g_ref, lb_ref, ng_ref, s0_ref, o_ref, sout_ref,
         st, b_s, k_s, qe_s, ke_s, o_s, u_s, stc_s) = refs
    else:
        (q_ref, f_ref, v_ref, g_ref, lb_ref, ng_ref, o_ref, sout_ref,
         st, b_s, k_s, qe_s, ke_s, o_s, u_s, stc_s) = refs
        s0_ref = None
    i = pl.program_id(2)
    tb, width = f_ref.shape[1], f_ref.shape[2]
    nchunk = tb // chunk
    grp = min(tb, HG_GROUP)
    levels = [w for w in (32, 16, 8, 4, 2, 1) if 2 * w <= chunk]
    assert chunk % HG_BLOCK == 0 and grp % chunk == 0 and tb % grp == 0
    heads = [slice(hh * HEAD_DIM, (hh + 1) * HEAD_DIM) for hh in range(n_heads)]

    if not chunk_is_sequence:
        @pl.when(i == 0)
        def _init():
            for hh in range(n_heads):
                st[hh] = s0_ref[0, hh].T if has_init else jnp.zeros((HEAD_DIM, HEAD_DIM), F32)

    lb = lb_ref[...]
    forget = lb + (1.0 - lb) * jax.nn.sigmoid(f_ref[0])
    kk = 1.0 - forget
    b = jnp.log(forget) * LOG2E
    row = lax.broadcasted_iota(jnp.int32, b.shape, 0) % chunk
    sh = 1
    while sh < chunk:
        b = b + jnp.where(row >= sh, pltpu.roll(b, sh, axis=0), 0.0)
        sh *= 2
    b3 = b.reshape(nchunk, chunk, width)
    b_end = jnp.broadcast_to(b3[:, chunk - 1:chunk, :], b3.shape).reshape(tb, width)
    b_s[...] = b
    k_s[...] = kk
    qe_s[...] = q_ref[0] * jnp.exp2(b)
    ke_s[...] = kk * jnp.exp2(b_end - b)

    rowi = lax.broadcasted_iota(jnp.int32, (grp, HEAD_DIM), 0)
    ti = lax.broadcasted_iota(jnp.int32, (grp, grp), 0)
    si = lax.broadcasted_iota(jnp.int32, (grp, grp), 1)

    def block_row(x, period, r):
        x3 = x.reshape(grp // period, period, HEAD_DIM)
        return jnp.broadcast_to(x3[:, r:r + 1, :], x3.shape).reshape(grp, HEAD_DIM)

    def neg_unless(cond):
        return jnp.where(cond, 0.0, -jnp.inf)

    diagonal = (ti == si).astype(F32)
    split = [(neg_unless(rowi % (2 * w) >= w), neg_unless(rowi % (2 * w) < w),
              (ti // (2 * w) == si // (2 * w)).astype(F32)) for w in levels]

    def split_row(x, w):
        if 2 * w >= HG_BLOCK:
            return block_row(x, 2 * w, w)
        out = block_row(x, HG_BLOCK, w)
        for start in range(2 * w, HG_BLOCK, 2 * w):
            out = jnp.where(rowi % HG_BLOCK >= start, block_row(x, HG_BLOCK, start + w), out)
        return out

    hp = 2 if n_heads % 2 == 0 else 1
    packs = [heads[i:i + hp] for i in range(0, n_heads, hp)]

    def side(mats):
        return mats[0] if len(mats) == 1 else jnp.concatenate(mats, axis=1)

    def block_diag(mats):
        if len(mats) == 1:
            return mats[0]
        a, b = mats
        return jnp.concatenate([jnp.concatenate([a, jnp.zeros_like(a)], axis=1),
                                jnp.concatenate([jnp.zeros_like(b), b], axis=1)], axis=0)

    diagonal = side([diagonal] * hp)
    split = [(later, earlier, side([same] * hp)) for later, earlier, same in split]

    def group_body(gi, carry):
        rs = pl.ds(pl.multiple_of(gi * grp, grp), grp)
        for pack in packs:
            qg = [q_ref[0, rs, cs] for cs in pack]
            kg = [k_s[rs, cs] for cs in pack]
            bg = [b_s[rs, cs] for cs in pack]
            kbd = block_diag([k.astype(BF16) for k in kg])
            a = lax.dot_general(side([q.astype(BF16) for q in qg]), kbd, _NT,
                                preferred_element_type=F32) * diagonal
            for w, (later, earlier, same_parent) in zip(levels, split):
                d = [b - split_row(b, w) for b in bg]
                qw = [(q * jnp.exp2(dd + later)).astype(BF16) for q, dd in zip(qg, d)]
                kw = [(k * jnp.exp2(earlier - dd)).astype(BF16) for k, dd in zip(kg, d)]
                a = a + lax.dot_general(side(qw), block_diag(kw), _NT,
                                        preferred_element_type=F32) * same_parent
            vbd = block_diag([v_ref[0, rs, cs].astype(BF16) for cs in pack])
            o_s[rs, pack[0].start:pack[-1].stop] = jnp.dot(a.astype(BF16), vbd, preferred_element_type=F32)
        return carry

    lax.fori_loop(0, tb // grp, group_body, 0, unroll=math.gcd(tb // grp, 4))

    for hh, cs in enumerate(heads):
        for c in range(nchunk):
            rs = slice(c * chunk, (c + 1) * chunk)
            u_s[c] = lax.dot_general(v_ref[0, rs, cs].astype(BF16), ke_s[rs, cs].astype(BF16), _TN,
                                     preferred_element_type=F32)
        s_cur = None if chunk_is_sequence else st[hh]
        for c in range(nchunk):
            if chunk_is_sequence:
                s_cur = s0_ref[c, hh].T
            stc_s[c] = s_cur.astype(BF16)
            last = (c + 1) * chunk - 1
            s_cur = s_cur * jnp.exp2(b_s[last:last + 1, cs]) + u_s[c]
            if chunk_is_sequence:
                sout_ref[c, hh] = s_cur.T
        if not chunk_is_sequence:
            st[hh] = s_cur
        for c in range(nchunk):
            rs = slice(c * chunk, (c + 1) * chunk)
            o_s[rs, cs] += lax.dot_general(qe_s[rs, cs].astype(BF16), stc_s[c], _NT,
                                           preferred_element_type=F32)

    for cs in heads:
        o = o_s[:, cs]
        y = o * lax.rsqrt(jnp.mean(o * o, axis=-1, keepdims=True) + NORM_EPS) * ng_ref[...]
        o_ref[0, :, cs] = (y * _silu(g_ref[0, :, cs])).astype(o_ref.dtype)

    if not chunk_is_sequence:
        @pl.when(i == pl.num_programs(2) - 1)
        def _emit():
            for hh in range(n_heads):
                sout_ref[0, hh] = st[hh].T


def _hgrn(z, sec0, lb, norm_g, s0, batch, length, n_heads, heads_per_step, tb, chunk, out_dtype, seqs_per_step=1):
    width = z.shape[2]
    wblk = heads_per_step * HEAD_DIM
    has_init = s0 is not None
    chunk_is_sequence = seqs_per_step > 1
    if chunk_is_sequence:
        assert has_init and tb == chunk == length and batch % seqs_per_step == 0
        batch, length, tb = batch // seqs_per_step, length * seqs_per_step, tb * seqs_per_step
    nblk = length // tb

    def sec_spec(sec):
        return pl.BlockSpec((1, tb, wblk), lambda b, h, i: (sec, b * nblk + i, h))

    in_specs = [sec_spec(sec0), sec_spec(sec0 + 1), sec_spec(sec0 + 2), sec_spec(sec0 + 3),
                pl.BlockSpec((1, wblk), lambda b, h, i: (0, h)),
                pl.BlockSpec((1, HEAD_DIM), lambda b, h, i: (0, 0))]
    args = [z, z, z, z, lb.reshape(1, width), norm_g.reshape(1, HEAD_DIM)]
    state_spec = pl.BlockSpec((seqs_per_step, heads_per_step, HEAD_DIM, HEAD_DIM), lambda b, h, i: (b, h, 0, 0))
    if has_init:
        in_specs.append(state_spec)
        args.append(s0)
    kern = functools.partial(_hgrn_kernel, chunk=chunk, n_heads=heads_per_step, has_init=has_init,
                             chunk_is_sequence=chunk_is_sequence)
    return pl.pallas_call(
        kern,
        out_shape=(jax.ShapeDtypeStruct((1, batch * length, width), out_dtype),
                   jax.ShapeDtypeStruct((batch * seqs_per_step, n_heads, HEAD_DIM, HEAD_DIM), F32)),
        grid=(batch, n_heads // heads_per_step, nblk),
        in_specs=in_specs,
        out_specs=(pl.BlockSpec((1, tb, wblk), lambda b, h, i: (0, b * nblk + i, h)), state_spec),
        scratch_shapes=[pltpu.VMEM((heads_per_step, HEAD_DIM, HEAD_DIM), F32)]
                       + [pltpu.VMEM((tb, wblk), F32)] * 5
                       + [pltpu.VMEM((tb // chunk, HEAD_DIM, HEAD_DIM), F32),
                          pltpu.VMEM((tb // chunk, HEAD_DIM, HEAD_DIM), BF16)],
        compiler_params=_params("arbitrary", "arbitrary", "arbitrary"),
        name="hgrn",
    )(*args)


def _outproj_kernel(ma_ref, mh_ref, w_ref, x_ref, rg_ref, fg_ref, y_ref, *, final_norm):
    half = ma_ref.shape[1]
    u = jnp.dot(ma_ref[...].astype(BF16), w_ref[:half, :], preferred_element_type=F32)
    u = u + jnp.dot(mh_ref[...].astype(BF16), w_ref[half:, :], preferred_element_type=F32)
    r = x_ref[...] + rg_ref[0] * u
    if final_norm:
        r = r * lax.rsqrt(jnp.mean(r * r, axis=-1, keepdims=True) + NORM_EPS) * fg_ref[...]
    y_ref[...] = r


def _outproj(m_att, m_hg, w_bf, x, rg, final_g, tm, rows_per_mod, final_norm):
    r, d = x.shape
    half = m_att.shape[1]
    rb = rg.shape[1]
    tiles_per_mod = rows_per_mod // tm
    return pl.pallas_call(
        functools.partial(_outproj_kernel, final_norm=final_norm),
        out_shape=jax.ShapeDtypeStruct((r, d), F32),
        grid=(r // tm,),
        in_specs=[pl.BlockSpec((tm, half), lambda i: (i, 0)),
                  pl.BlockSpec((tm, half), lambda i: (i, 0)),
                  pl.BlockSpec((2 * half, d), lambda i: (0, 0)),
                  pl.BlockSpec((tm, d), lambda i: (i, 0)),
                  pl.BlockSpec((1, rb, d), lambda i: (i // tiles_per_mod, 0, 0)),
                  pl.BlockSpec((1, d), lambda i: (0, 0))],
        out_specs=pl.BlockSpec((tm, d), lambda i: (i, 0)),
        compiler_params=_params("arbitrary"),
        name="outproj",
    )(m_att, m_hg, w_bf, x, rg, final_g.reshape(1, d))


def _tile_plan(seq, n_pages):
    return {"rows": math.gcd(seq, 1024), "out_rows": math.gcd(seq, 512),
            "pages": math.gcd(n_pages, SAMPLE_PAGES_PER_STEP)}


def kernel(x_prompt, x_sample, cache_k, cache_v, state_hgrn, page_table, c_prompt, c_sample,
           norm_g, w_ada, b_ada, w_in, lambda_q1, lambda_k1, lambda_q2, lambda_k2, subln_g,
           hg_lower_bounds, hg_norm_g, w_out, final_g):
    batch, seq, d = x_prompt.shape
    dec_batch, n_new, _ = x_sample.shape
    depth, n_pool, page, n_heads, _, qk_dim = cache_k.shape
    hg_heads = state_hgrn.shape[2]
    width = d // 2
    assert qk_dim == QK_DIM and n_heads * HEAD_DIM == width and hg_heads * HEAD_DIM == width
    assert w_in.shape[2] == N_SECTIONS * width
    tiles = _tile_plan(seq, page_table.shape[1])

    slopes = jnp.asarray([2.0 ** (-8.0 * (h + 1) / n_heads) for h in range(n_heads)], F32)
    lb_all = jnp.cumsum(jax.nn.softmax(hg_lower_bounds.astype(F32), axis=0), axis=0)
    n_mod = batch + dec_batch
    n_mod_pad = -(-n_mod // 8) * 8
    c_all = jnp.concatenate([c_prompt, c_sample, jnp.zeros((n_mod_pad - n_mod, d), F32)], axis=0)

    xp = x_prompt.reshape(batch * seq, d)
    xs = x_sample.reshape(dec_batch * n_new, d)
    rs = dec_batch * n_new
    outs = [[] for _ in range(6)]
    for l in range(depth):
        last = l == depth - 1
        lam_init = 0.8 - 0.6 * math.exp(-0.3 * l)
        lam = (jnp.exp(jnp.sum(lambda_q1[l].astype(F32) * lambda_k1[l].astype(F32)))
               - jnp.exp(jnp.sum(lambda_q2[l].astype(F32) * lambda_k2[l].astype(F32))) + lam_init)
        par = jnp.concatenate([lam.reshape(1), slopes, jnp.zeros((7,), F32)])
        w_out_bf = w_out[l].astype(BF16)

        mod = _adaln(c_all, w_ada[l], b_ada[l])
        shift, scale, rgate = mod[:, :d], mod[:, d:2 * d], mod[:, 2 * d:]

        def rows_p(t):
            return t[:batch].reshape(batch, 1, d)

        def rows_s(t):
            return jnp.repeat(t[batch:n_mod], n_new, axis=0).reshape(1, rs, d)

        t = tiles
        hs = _modnorm(xs, norm_g[l], rows_s(scale), rows_s(shift), tm=rs, rows_per_mod=rs)
        zs, w_in_bf = _inproj(hs, w_in[l])
        hp = _modnorm(xp, norm_g[l], rows_p(scale), rows_p(shift), tm=t["rows"], rows_per_mod=seq)
        qkv, gh, k_t, v_rows = _inproj_prompt(hp, w_in_bf, batch, seq, n_heads, tm=t["rows"])
        cache_kt = jnp.transpose(cache_k[l], (0, 2, 3, 4, 1)).reshape(n_pool, width, page)
        cache_vr = cache_v[l].reshape(n_pool, page * n_heads, HEAD_DIM)
        att_s = _attn_sample(zs.reshape(N_SECTIONS, dec_batch, n_new, width), cache_kt, cache_vr,
                             page_table, par, subln_g[l], n_heads, n_new, 1.0 - lam_init, npg=t["pages"])
        att_p = _attn_prompt(qkv, gh, par, subln_g[l], batch, seq, n_heads, 1.0 - lam_init,
                             tq=t["rows"], tk=t["rows"] // 2)
        hg_p, s_p = _hgrn(gh, 1, lb_all[l], hg_norm_g[l], None, batch, seq, hg_heads,
                          heads_per_step=2 if hg_heads % 2 == 0 else 1,
                          tb=t["rows"], chunk=math.gcd(seq, HG_CHUNK), out_dtype=BF16)
        xp = _outproj(att_p, hg_p[0], w_out_bf, xp, rows_p(rgate), final_g,
                      tm=t["out_rows"], rows_per_mod=seq, final_norm=last)
        k_p = jnp.transpose(k_t.reshape(batch, n_heads, 2, QK_DIM, seq), (0, 4, 1, 2, 3))
        outs[0].append(k_p)
        outs[1].append(v_rows.reshape(batch, seq, n_heads, HEAD_DIM))
        outs[2].append(s_p)

        hg_s, s_s = _hgrn(zs, 4, lb_all[l], hg_norm_g[l], state_hgrn[l], dec_batch, n_new, hg_heads,
                          heads_per_step=hg_heads, tb=n_new, chunk=n_new, out_dtype=F32,
                          seqs_per_step=math.gcd(dec_batch, 8))
        xs = _outproj(att_s.reshape(rs, width), hg_s[0], w_out_bf, xs, rows_s(rgate), final_g,
                      tm=rs, rows_per_mod=rs, final_norm=last)
        outs[3].append(zs[1].reshape(dec_batch, n_new, n_heads, 2, QK_DIM))
        outs[4].append(zs[2].reshape(dec_batch, n_new, n_heads, HEAD_DIM))
        outs[5].append(s_s)

    y_prompt = xp.reshape(batch, seq, d)
    y_sample = xs.reshape(dec_batch, n_new, d)
    return (y_prompt, y_sample) + tuple(jnp.stack(o) for o in outs)
```

```python
import functools
import math

import jax
import jax.numpy as jnp
from jax import lax
from jax.experimental import pallas as pl
from jax.experimental.pallas import tpu as pltpu

F32 = jnp.float32
BF16 = jnp.bfloat16

NORM_EPS = 1e-6
SUBLN_EPS = 1e-5
N_SECTIONS = 8
HEAD_DIM = 128
QK_DIM = 64
HG_CHUNK = 64
HG_BLOCK = 8
HG_GROUP = 128
LOG2E = 1.4426950408889634
SAMPLE_PAGES_PER_STEP = 16
VMEM_LIMIT_BYTES = 56 * 1024 * 1024

_NT = (((1,), (1,)), ((), ()))
_TN = (((0,), (0,)), ((), ()))


def _params(*sem):
    return pltpu.CompilerParams(dimension_semantics=sem, vmem_limit_bytes=VMEM_LIMIT_BYTES)


def _silu(x):
    return x * jax.nn.sigmoid(x)


def _adaln_kernel(c_ref, w_ref, b_ref, o_ref):
    a = _silu(c_ref[...]).astype(BF16)
    o_ref[...] = jnp.dot(a, w_ref[...].astype(BF16), preferred_element_type=F32) + b_ref[...]


def _adaln(c, w, b, tn=1024):
    m, d = c.shape
    n = w.shape[1]
    return pl.pallas_call(
        _adaln_kernel,
        out_shape=jax.ShapeDtypeStruct((m, n), F32),
        grid=(n // tn,),
        in_specs=[pl.BlockSpec((m, d), lambda j: (0, 0)),
                  pl.BlockSpec((d, tn), lambda j: (0, j)),
                  pl.BlockSpec((1, tn), lambda j: (0, j))],
        out_specs=pl.BlockSpec((m, tn), lambda j: (0, j)),
        compiler_params=_params("arbitrary"),
        name="adaln",
    )(c, w, b.reshape(1, n))


MODNORM_ROWS = 16


def _modnorm_kernel(x_ref, g_ref, sc_ref, sh_ref, h_ref):
    per_row_mod = sc_ref.shape[1] > 1

    def body(c, carry):
        rs = pl.ds(pl.multiple_of(c * MODNORM_ROWS, MODNORM_ROWS), MODNORM_ROWS)
        x = x_ref[rs, :]
        y = x * lax.rsqrt(jnp.mean(x * x, axis=-1, keepdims=True) + NORM_EPS) * g_ref[...]
        sc = sc_ref[0, rs, :] if per_row_mod else sc_ref[0]
        sh = sh_ref[0, rs, :] if per_row_mod else sh_ref[0]
        h_ref[rs, :] = (y * (1.0 + sc) + sh).astype(h_ref.dtype)
        return carry

    lax.fori_loop(0, x_ref.shape[0] // MODNORM_ROWS, body, 0, unroll=8)


def _modnorm(x, g, scale, shift, tm, rows_per_mod):
    r, d = x.shape
    rb = scale.shape[1]
    tiles_per_mod = rows_per_mod // tm
    mod_spec = pl.BlockSpec((1, rb, d), lambda i: (i // tiles_per_mod, 0, 0))
    return pl.pallas_call(
        _modnorm_kernel,
        out_shape=jax.ShapeDtypeStruct((r, d), BF16),
        grid=(r // tm,),
        in_specs=[pl.BlockSpec((tm, d), lambda i: (i, 0)),
                  pl.BlockSpec((1, d), lambda i: (0, 0)),
                  mod_spec, mod_spec],
        out_specs=pl.BlockSpec((tm, d), lambda i: (i, 0)),
        compiler_params=_params("arbitrary"),
        name="modnorm",
    )(x, g.reshape(1, d), scale, shift)


def _inproj_kernel(h_ref, w_ref, o_ref, wb_ref):
    wb = w_ref[...].astype(BF16)
    wb_ref[...] = wb
    o_ref[0] = jnp.dot(h_ref[...], wb, preferred_element_type=F32)


def _inproj(h, w):
    r, d = h.shape
    wsec = w.shape[1] // N_SECTIONS
    return pl.pallas_call(
        _inproj_kernel,
        out_shape=(jax.ShapeDtypeStruct((N_SECTIONS, r, wsec), F32),
                   jax.ShapeDtypeStruct(w.shape, BF16)),
        grid=(N_SECTIONS,),
        in_specs=[pl.BlockSpec((r, d), lambda s: (0, 0)),
                  pl.BlockSpec((d, wsec), lambda s: (0, s))],
        out_specs=(pl.BlockSpec((1, r, wsec), lambda s: (s, 0, 0)),
                   pl.BlockSpec((d, wsec), lambda s: (0, s))),
        compiler_params=_params("arbitrary"),
        name="inproj",
    )(h, w)


def _inproj_prompt_kernel(h_ref, w_ref, qkv_ref, gh_ref, kt_ref, vo_ref, *, n_heads):
    s = pl.program_id(1)
    tm = h_ref.shape[0]
    z = jnp.dot(h_ref[...], w_ref[...], preferred_element_type=F32)

    @pl.when(s <= 2)
    def _():
        qkv_ref[0] = z.astype(BF16)

    @pl.when(s == 1)
    def _():
        kt_ref[0] = z.T

    @pl.when(s == 2)
    def _():
        for hh in range(n_heads):
            vo_ref[pl.ds(hh, tm, stride=n_heads), :] = z[:, hh * HEAD_DIM:(hh + 1) * HEAD_DIM]

    @pl.when(s >= 3)
    def _():
        gh_ref[0] = z


def _inproj_prompt(h, w, batch, seq, n_heads, tm):
    r, d = h.shape
    wsec = w.shape[1] // N_SECTIONS
    tpb = seq // tm
    kern = functools.partial(_inproj_prompt_kernel, n_heads=n_heads)
    return pl.pallas_call(
        kern,
        out_shape=(jax.ShapeDtypeStruct((3, r, wsec), BF16),
                   jax.ShapeDtypeStruct((N_SECTIONS - 3, r, wsec), F32),
                   jax.ShapeDtypeStruct((batch, wsec, seq), F32),
                   jax.ShapeDtypeStruct((r * n_heads, HEAD_DIM), F32)),
        grid=(r // tm, N_SECTIONS),
        in_specs=[pl.BlockSpec((tm, d), lambda i, s: (i, 0)),
                  pl.BlockSpec((d, wsec), lambda i, s: (0, s))],
        out_specs=(pl.BlockSpec((1, tm, wsec), lambda i, s: (jnp.minimum(s, 2), i, 0)),
                   pl.BlockSpec((1, tm, wsec), lambda i, s: (jnp.maximum(s, 3) - 3, i, 0)),
                   pl.BlockSpec((1, wsec, tm), lambda i, s: (i // tpb, 0, i % tpb)),
                   pl.BlockSpec((tm * n_heads, HEAD_DIM), lambda i, s: (i, 0))),
        compiler_params=_params("arbitrary", "arbitrary"),
        name="inproj_prompt",
    )(h, w)


def _prompt_attn_ops(par_ref, q_ref, k_ref, v_ref, g_ref, sg_ref, o_ref,
                     vt, kx_s, mask_s, sx, sy, m_s, acc_s, *, h, qt, tq, tk, out_scale):
    lam = par_ref[0]
    slope2 = par_ref[1 + h] * LOG2E
    assert tq == 2 * tk

    def fill():
        ones_row = (lax.broadcasted_iota(jnp.int32, (vt.shape[1] - HEAD_DIM, tk), 0) == 0).astype(BF16)
        for j in range(k_ref.shape[1] // tk):
            vt[j, :HEAD_DIM, :] = v_ref[0, j * tk:(j + 1) * tk, :].astype(F32).T.astype(BF16)
            vt[j, HEAD_DIM:, :] = ones_row
        jl = lax.broadcasted_iota(jnp.int32, (tk, HEAD_DIM), 0).astype(F32) * slope2
        hi = jl.astype(BF16).astype(F32)
        mid = (jl - hi).astype(BF16).astype(F32)
        lo = jl - hi - mid
        lane = lax.broadcasted_iota(jnp.int32, (tk, HEAD_DIM), 1)
        kx_s[...] = jnp.where(lane == 0, hi, jnp.where(lane == 1, mid, jnp.where(lane == 2, lo, 0.0))).astype(BF16)
        jj = lax.broadcasted_iota(jnp.int32, (tk, tq), 0)
        ii = lax.broadcasted_iota(jnp.int32, (tk, tq), 1)
        mask_s[...] = jnp.where(ii >= jj, 0.0, -jnp.inf)

    q = q_ref[0].astype(F32) * (QK_DIM ** -0.5 * LOG2E)
    lane = lax.broadcasted_iota(jnp.int32, q.shape, 1)
    qx = (lane < 3).astype(F32).astype(BF16)
    qs = (jnp.concatenate([jnp.where(lane < QK_DIM, q, 0.0).astype(BF16), qx], axis=1),
          jnp.concatenate([jnp.where(lane >= QK_DIM, q, 0.0).astype(BF16), qx], axis=1))
    c0 = -slope2 * lax.broadcasted_iota(jnp.int32, (1, tq), 1).astype(F32)

    whole = slice(0, tq)

    def qk(kt, j, cols=whole):
        keys = k_ref[0, pl.ds(pl.multiple_of(kt * tk, tk), tk), :]
        return lax.dot_general(jnp.concatenate([keys, kx_s[...]], axis=1), qs[j][cols], _NT,
                               preferred_element_type=F32)

    def update(j, s, kt, cols=whole):
        c = c0[:, cols] + slope2 * (jnp.zeros((1, s.shape[1]), jnp.int32) + (kt * tk - qt * tq)).astype(F32)
        m_old = m_s[j, :, cols]
        m_new = jnp.maximum(m_old, jnp.max(s, axis=0, keepdims=True) + c)
        alpha = jnp.exp2(m_old - m_new)
        p = jnp.exp2(s - (m_new - c))
        m_s[j, :, cols] = m_new
        acc_s[j, :, cols] = alpha * acc_s[j, :, cols] + jnp.dot(vt[kt], p.astype(BF16),
                                                                preferred_element_type=F32)

    def start():
        m_s[...] = jnp.full(m_s.shape, -jnp.inf, F32)
        acc_s[...] = jnp.zeros(acc_s.shape, F32)
        for j in range(2):
            sx[j] = qk(0, j)

    def step(u):
        ka = 2 * u
        for j in range(2):
            sy[j] = qk(ka + 1, j)
        for j in range(2):
            update(j, sx[j], ka)
        for j in range(2):
            sx[j] = qk(ka + 2, j)
        for j in range(2):
            update(j, sy[j], ka + 1)

    def finish():
        kd = 2 * qt
        late = slice(tk, tq)
        for j in range(2):
            sy[j, :, late] = qk(kd + 1, j, late)
        for j in range(2):
            update(j, sx[j] + mask_s[...], kd)
        for j in range(2):
            update(j, sy[j, :, late] + mask_s[:, :tk], kd + 1, late)
        o0 = acc_s[0]
        o1 = acc_s[1]
        od = (o0[:HEAD_DIM] / o0[HEAD_DIM:HEAD_DIM + 1] - lam * (o1[:HEAD_DIM] / o1[HEAD_DIM:HEAD_DIM + 1])).T
        y = od * lax.rsqrt(jnp.mean(od * od, axis=-1, keepdims=True) + SUBLN_EPS) * sg_ref[...] * out_scale
        o_ref[...] = (y * _silu(g_ref[0])).astype(o_ref.dtype)

    return fill, start, step, finish


def _prompt_attn_scratch(seq, tq, tk):
    return [pltpu.VMEM((seq // tk, HEAD_DIM + 16, tk), BF16),
            pltpu.VMEM((tk, HEAD_DIM), BF16),
            pltpu.VMEM((tk, tq), F32),
            pltpu.VMEM((2, tk, tq), F32),
            pltpu.VMEM((2, tk, tq), F32),
            pltpu.VMEM((2, 1, tq), F32),
            pltpu.VMEM((2, HEAD_DIM + 16, tq), F32)]


def _attn_prompt_kernel(*refs, tq, tk, out_scale):
    qt = pl.program_id(2)
    fill, start, step, finish = _prompt_attn_ops(*refs, h=pl.program_id(1), qt=qt, tq=tq, tk=tk,
                                                 out_scale=out_scale)
    pl.when(qt == 0)(fill)
    start()
    lax.fori_loop(0, qt, lambda u, c: (step(u), c)[1], 0)
    finish()


def _attn_prompt(qkv, gh, par, subln_g, batch, seq, n_heads, out_scale, tq=1024, tk=512):
    nq = seq // tq
    kern = functools.partial(_attn_prompt_kernel, tq=tq, tk=tk, out_scale=out_scale)
    return pl.pallas_call(
        kern,
        out_shape=jax.ShapeDtypeStruct((batch * seq, n_heads * HEAD_DIM), BF16),
        grid=(batch, n_heads, nq),
        in_specs=[pl.BlockSpec(memory_space=pltpu.SMEM),
                  pl.BlockSpec((1, tq, HEAD_DIM), lambda b, h, i: (0, b * nq + i, h)),
                  pl.BlockSpec((1, seq, HEAD_DIM), lambda b, h, i: (1, b, h)),
                  pl.BlockSpec((1, seq, HEAD_DIM), lambda b, h, i: (2, b, h)),
                  pl.BlockSpec((1, tq, HEAD_DIM), lambda b, h, i: (0, b * nq + i, h)),
                  pl.BlockSpec((1, HEAD_DIM), lambda b, h, i: (0, 0))],
        out_specs=pl.BlockSpec((tq, HEAD_DIM), lambda b, h, i: (b * nq + i, h)),
        scratch_shapes=_prompt_attn_scratch(seq, tq, tk),
        compiler_params=_params("arbitrary", "arbitrary", "arbitrary"),
        name="attn_prompt",
    )(par, qkv, qkv, qkv, gh, subln_g.reshape(1, HEAD_DIM))


def _attn_sample_kernel(pt_ref, par_ref, slope_ref, q_ref, kn_ref, vn_ref, g_ref, sg_ref, *rest,
                        npg, page, n_new, n_heads, out_scale):
    k_refs = rest[:npg]
    v_refs = rest[npg:2 * npg]
    o_ref = rest[2 * npg]
    qbd, b0_s, m_s, l_s, acc_s = rest[2 * npg + 1:]
    i = pl.program_id(1)
    n_steps = pl.num_programs(1)
    tblk = npg * page
    nrow, width = qbd.shape
    rows_per_head = 2 * n_new
    slope = slope_ref[...] * LOG2E

    def head_rows(hh):
        return slice(hh * rows_per_head, (hh + 1) * rows_per_head)

    @pl.when(i == 0)
    def _init():
        q = q_ref[0] * (QK_DIM ** -0.5 * LOG2E)
        qt = jnp.concatenate([q] * (nrow // n_new), axis=0)
        rr = lax.broadcasted_iota(jnp.int32, (nrow, width), 0)
        cc = lax.broadcasted_iota(jnp.int32, (nrow, width), 1)
        qbd[...] = jnp.where(rr // n_new == cc // QK_DIM, qt, 0.0).astype(BF16)
        b0_s[...] = slope * lax.broadcasted_iota(jnp.int32, (nrow, tblk), 1).astype(F32)
        zpad = jnp.zeros((nrow - n_new, width), F32)
        kn = jnp.concatenate([kn_ref[0], zpad], axis=0).astype(BF16)
        vn = jnp.concatenate([vn_ref[0], zpad], axis=0).astype(BF16)
        s = lax.dot_general(qbd[...], kn, _NT, preferred_element_type=F32)
        tk = lax.broadcasted_iota(jnp.int32, (nrow, nrow), 1)
        tqry = lax.broadcasted_iota(jnp.int32, (nrow, nrow), 0) % n_new
        s = jnp.where(tk <= tqry, s + slope * tk.astype(F32), -jnp.inf)
        m = jnp.max(s, axis=1, keepdims=True)
        p = jnp.exp2(s - m)
        m_s[...] = m
        l_s[...] = jnp.sum(p, axis=1, keepdims=True)
        pb = p.astype(BF16)
        for hh in range(n_heads):
            acc_s[head_rows(hh), :] = jnp.dot(pb[head_rows(hh), :], vn[:, hh * HEAD_DIM:(hh + 1) * HEAD_DIM],
                                              preferred_element_type=F32)

    gp = 2 if npg % 2 == 0 else 1
    qb = qbd[...]
    s = jnp.concatenate(
        [jnp.dot(qb, jnp.concatenate([k_refs[pg + t][...].astype(BF16) for t in range(gp)], axis=1),
                 preferred_element_type=F32) for pg in range(0, npg, gp)], axis=1) + b0_s[...]
    c = slope * (jnp.zeros((nrow, 1), jnp.int32) + (i - n_steps) * tblk).astype(F32)
    m_old = m_s[...]
    m_new = jnp.maximum(m_old, jnp.max(s, axis=1, keepdims=True) + c)
    alpha = jnp.exp2(m_old - m_new)
    p = jnp.exp2(s - (m_new - c))
    m_s[...] = m_new
    l_s[...] = alpha * l_s[...] + jnp.sum(p, axis=1, keepdims=True)
    pb = p.astype(BF16)
    for hh in range(n_heads):
        rs = head_rows(hh)
        pv = jnp.zeros((rows_per_head, HEAD_DIM), F32)
        for pg in range(0, npg, gp):
            vb = jnp.concatenate([v_refs[pg + t][pl.ds(hh, page, stride=n_heads), :].astype(BF16)
                                  for t in range(gp)], axis=0)
            pv = pv + jnp.dot(pb[rs, pg * page:(pg + gp) * page], vb, preferred_element_type=F32)
        acc_s[rs, :] = alpha[rs, :] * acc_s[rs, :] + pv

    @pl.when(i == n_steps - 1)
    def _finish():
        lam = par_ref[0]
        o = acc_s[...] / l_s[...]
        g = g_ref[0]
        for hh in range(n_heads):
            cs = slice(hh * HEAD_DIM, (hh + 1) * HEAD_DIM)
            blk = o[head_rows(hh), :]
            od = blk[:n_new] - lam * blk[n_new:]
            y = od * lax.rsqrt(jnp.mean(od * od, axis=-1, keepdims=True) + SUBLN_EPS) * sg_ref[...] * out_scale
            o_ref[0, :, cs] = y * _silu(g[:, cs])

def _attn_sample(z, cache_kt, cache_v, page_table, par, subln_g, n_heads, n_new, out_scale, npg):
    dec_batch, n_pages = page_table.shape
    width, page = cache_kt.shape[1], cache_kt.shape[2]
    nrow = n_heads * 2 * n_new
    assert nrow == HEAD_DIM and n_pages % npg == 0
    tblk = npg * page
    n_steps = n_pages // npg
    row_head = jnp.arange(nrow) // (2 * n_new)
    slope_col = jnp.exp2(-8.0 * (row_head + 1).astype(F32) / n_heads).reshape(nrow, 1)

    def page_spec(pg, rows, cols):
        return pl.BlockSpec((None, rows, cols), lambda s, i, pt: (pt[s, i * npg + pg], 0, 0))

    def sec_spec(sec):
        return pl.BlockSpec((1, None, n_new, width), lambda s, i, pt: (sec, s, 0, 0))

    kern = functools.partial(_attn_sample_kernel, npg=npg, page=page, n_new=n_new, n_heads=n_heads,
                             out_scale=out_scale)
    grid_spec = pltpu.PrefetchScalarGridSpec(
        num_scalar_prefetch=1,
        grid=(dec_batch, n_steps),
        in_specs=[pl.BlockSpec(memory_space=pltpu.SMEM),
                  pl.BlockSpec((nrow, 1), lambda s, i, pt: (0, 0)),
                  sec_spec(0), sec_spec(1), sec_spec(2), sec_spec(3),
                  pl.BlockSpec((1, HEAD_DIM), lambda s, i, pt: (0, 0))]
                 + [page_spec(pg, width, page) for pg in range(npg)]
                 + [page_spec(pg, page * n_heads, HEAD_DIM) for pg in range(npg)],
        out_specs=pl.BlockSpec((1, n_new, width), lambda s, i, pt: (s, 0, 0)),
        scratch_shapes=[pltpu.VMEM((nrow, width), BF16),
                        pltpu.VMEM((nrow, tblk), F32),
                        pltpu.VMEM((nrow, 1), F32),
                        pltpu.VMEM((nrow, 1), F32),
                        pltpu.VMEM((nrow, HEAD_DIM), F32)],
    )
    return pl.pallas_call(
        kern,
        out_shape=jax.ShapeDtypeStruct((dec_batch, n_new, width), F32),
        grid_spec=grid_spec,
        compiler_params=_params("arbitrary", "arbitrary"),
        name="attn_sample",
    )(page_table, par, slope_col, z, z, z, z, subln_g.reshape(1, HEAD_DIM),
      *([cache_kt] * npg), *([cache_v] * npg))


def _hgrn_kernel(*refs, chunk, n_heads, has_init, chunk_is_sequence):
    if has_init:
        (q_ref, f_ref, v_ref, g_ref, lb_ref, ng_ref, s0_ref, o_ref, sout_ref,
         st, b_s, k_s, qe_s, ke_s, o_s, u_s, stc_s) = refs
    else:
        (q_ref, f_ref, v_ref, g_ref, lb_ref, ng_ref, o_ref, sout_ref,
         st, b_s, k_s, qe_s, ke_s, o_s, u_s, stc_s) = refs
        s0_ref = None
    i = pl.program_id(2)
    tb, width = f_ref.shape[1], f_ref.shape[2]
    nchunk = tb // chunk
    grp = min(tb, HG_GROUP)
    levels = [w for w in (32, 16, 8, 4, 2, 1) if 2 * w <= chunk]
    assert chunk % HG_BLOCK == 0 and grp % chunk == 0 and tb % grp == 0
    heads = [slice(hh * HEAD_DIM, (hh + 1) * HEAD_DIM) for hh in range(n_heads)]

    if not chunk_is_sequence:
        @pl.when(i == 0)
        def _init():
            for hh in range(n_heads):
                st[hh] = s0_ref[0, hh].T if has_init else jnp.zeros((HEAD_DIM, HEAD_DIM), F32)

    lb = lb_ref[...]
    forget = lb + (1.0 - lb) * jax.nn.sigmoid(f_ref[0])
    kk = 1.0 - forget
    b = jnp.log(forget) * LOG2E
    row = lax.broadcasted_iota(jnp.int32, b.shape, 0) % chunk
    sh = 1
    while sh < chunk:
        b = b + jnp.where(row >= sh, pltpu.roll(b, sh, axis=0), 0.0)
        sh *= 2
    b3 = b.reshape(nchunk, chunk, width)
    b_end = jnp.broadcast_to(b3[:, chunk - 1:chunk, :], b3.shape).reshape(tb, width)
    b_s[...] = b
    k_s[...] = kk
    qe_s[...] = q_ref[0] * jnp.exp2(b)
    ke_s[...] = kk * jnp.exp2(b_end - b)

    rowi = lax.broadcasted_iota(jnp.int32, (grp, HEAD_DIM), 0)
    ti = lax.broadcasted_iota(jnp.int32, (grp, grp), 0)
    si = lax.broadcasted_iota(jnp.int32, (grp, grp), 1)

    def block_row(x, period, r):
        x3 = x.reshape(grp // period, period, HEAD_DIM)
        return jnp.broadcast_to(x3[:, r:r + 1, :], x3.shape).reshape(grp, HEAD_DIM)

    def neg_unless(cond):
        return jnp.where(cond, 0.0, -jnp.inf)

    diagonal = (ti == si).astype(F32)
    split = [(neg_unless(rowi % (2 * w) >= w), neg_unless(rowi % (2 * w) < w),
              (ti // (2 * w) == si // (2 * w)).astype(F32)) for w in levels]

    def split_row(x, w):
        if 2 * w >= HG_BLOCK:
            return block_row(x, 2 * w, w)
        out = block_row(x, HG_BLOCK, w)
        for start in range(2 * w, HG_BLOCK, 2 * w):
            out = jnp.where(rowi % HG_BLOCK >= start, block_row(x, HG_BLOCK, start + w), out)
        return out

    hp = 2 if n_heads % 2 == 0 else 1
    packs = [heads[i:i + hp] for i in range(0, n_heads, hp)]

    def side(mats):
        return mats[0] if len(mats) == 1 else jnp.concatenate(mats, axis=1)

    def block_diag(mats):
        if len(mats) == 1:
            return mats[0]
        a, b = mats
        return jnp.concatenate([jnp.concatenate([a, jnp.zeros_like(a)], axis=1),
                                jnp.concatenate([jnp.zeros_like(b), b], axis=1)], axis=0)

    diagonal = side([diagonal] * hp)
    split = [(later, earlier, side([same] * hp)) for later, earlier, same in split]

    def group_body(gi, carry):
        rs = pl.ds(pl.multiple_of(gi * grp, grp), grp)
        for pack in packs:
            qg = [q_ref[0, rs, cs] for cs in pack]
            kg = [k_s[rs, cs] for cs in pack]
            bg = [b_s[rs, cs] for cs in pack]
            kbd = block_diag([k.astype(BF16) for k in kg])
            a = lax.dot_general(side([q.astype(BF16) for q in qg]), kbd, _NT,
                                preferred_element_type=F32) * diagonal
            for w, (later, earlier, same_parent) in zip(levels, split):
                d = [b - split_row(b, w) for b in bg]
                qw = [(q * jnp.exp2(dd + later)).astype(BF16) for q, dd in zip(qg, d)]
                kw = [(k * jnp.exp2(earlier - dd)).astype(BF16) for k, dd in zip(kg, d)]
                a = a + lax.dot_general(side(qw), block_diag(kw), _NT,
                                        preferred_element_type=F32) * same_parent
            vbd = block_diag([v_ref[0, rs, cs].astype(BF16) for cs in pack])
            o_s[rs, pack[0].start:pack[-1].stop] = jnp.dot(a.astype(BF16), vbd, preferred_element_type=F32)
        return carry

    lax.fori_loop(0, tb // grp, group_body, 0, unroll=math.gcd(tb // grp, 4))

    for hh, cs in enumerate(heads):
        for c in range(nchunk):
            rs = slice(c * chunk, (c + 1) * chunk)
            u_s[c] = lax.dot_general(v_ref[0, rs, cs].astype(BF16), ke_s[rs, cs].astype(BF16), _TN,
                                     preferred_element_type=F32)
        s_cur = None if chunk_is_sequence else st[hh]
        for c in range(nchunk):
            if chunk_is_sequence:
                s_cur = s0_ref[c, hh].T
            stc_s[c] = s_cur.astype(BF16)
            last = (c + 1) * chunk - 1
            s_cur = s_cur * jnp.exp2(b_s[last:last + 1, cs]) + u_s[c]
            if chunk_is_sequence:
                sout_ref[c, hh] = s_cur.T
        if not chunk_is_sequence:
            st[hh] = s_cur
        for c in range(nchunk):
            rs = slice(c * chunk, (c + 1) * chunk)
            o_s[rs, cs] += lax.dot_general(qe_s[rs, cs].astype(BF16), stc_s[c], _NT,
                                           preferred_element_type=F32)

    for cs in heads:
        o = o_s[:, cs]
        y = o * lax.rsqrt(jnp.mean(o * o, axis=-1, keepdims=True) + NORM_EPS) * ng_ref[...]
        o_ref[0, :, cs] = (y * _silu(g_ref[0, :, cs])).astype(o_ref.dtype)

    if not chunk_is_sequence:
        @pl.when(i == pl.num_programs(2) - 1)
        def _emit():
            for hh in range(n_heads):
                sout_ref[0, hh] = st[hh].T


def _hgrn(z, sec0, lb, norm_g, s0, batch, length, n_heads, heads_per_step, tb, chunk, out_dtype, seqs_per_step=1):
    width = z.shape[2]
    wblk = heads_per_step * HEAD_DIM
    has_init = s0 is not None
    chunk_is_sequence = seqs_per_step > 1
    if chunk_is_sequence:
        assert has_init and tb == chunk == length and batch % seqs_per_step == 0
        batch, length, tb = batch // seqs_per_step, length * seqs_per_step, tb * seqs_per_step
    nblk = length // tb

    def sec_spec(sec):
        return pl.BlockSpec((1, tb, wblk), lambda b, h, i: (sec, b * nblk + i, h))

    in_specs = [sec_spec(sec0), sec_spec(sec0 + 1), sec_spec(sec0 + 2), sec_spec(sec0 + 3),
                pl.BlockSpec((1, wblk), lambda b, h, i: (0, h)),
                pl.BlockSpec((1, HEAD_DIM), lambda b, h, i: (0, 0))]
    args = [z, z, z, z, lb.reshape(1, width), norm_g.reshape(1, HEAD_DIM)]
    state_spec = pl.BlockSpec((seqs_per_step, heads_per_step, HEAD_DIM, HEAD_DIM), lambda b, h, i: (b, h, 0, 0))
    if has_init:
        in_specs.append(state_spec)
        args.append(s0)
    kern = functools.partial(_hgrn_kernel, chunk=chunk, n_heads=heads_per_step, has_init=has_init,
                             chunk_is_sequence=chunk_is_sequence)
    return pl.pallas_call(
        kern,
        out_shape=(jax.ShapeDtypeStruct((1, batch * length, width), out_dtype),
                   jax.ShapeDtypeStruct((batch * seqs_per_step, n_heads, HEAD_DIM, HEAD_DIM), F32)),
        grid=(batch, n_heads // heads_per_step, nblk),
        in_specs=in_specs,
        out_specs=(pl.BlockSpec((1, tb, wblk), lambda b, h, i: (0, b * nblk + i, h)), state_spec),
        scratch_shapes=[pltpu.VMEM((heads_per_step, HEAD_DIM, HEAD_DIM), F32)]
                       + [pltpu.VMEM((tb, wblk), F32)] * 5
                       + [pltpu.VMEM((tb // chunk, HEAD_DIM, HEAD_DIM), F32),
                          pltpu.VMEM((tb // chunk, HEAD_DIM, HEAD_DIM), BF16)],
        compiler_params=_params("arbitrary", "arbitrary", "arbitrary"),
        name="hgrn",
    )(*args)


def _outproj_kernel(ma_ref, mh_ref, w_ref, x_ref, rg_ref, fg_ref, y_ref, *, final_norm):
    half = ma_ref.shape[1]
    u = jnp.dot(ma_ref[...].astype(BF16), w_ref[:half, :], preferred_element_type=F32)
    u = u + jnp.dot(mh_ref[...].astype(BF16), w_ref[half:, :], preferred_element_type=F32)
    r = x_ref[...] + rg_ref[0] * u
    if final_norm:
        r = r * lax.rsqrt(jnp.mean(r * r, axis=-1, keepdims=True) + NORM_EPS) * fg_ref[...]
    y_ref[...] = r


def _outproj(m_att, m_hg, w_bf, x, rg, final_g, tm, rows_per_mod, final_norm):
    r, d = x.shape
    half = m_att.shape[1]
    rb = rg.shape[1]
    tiles_per_mod = rows_per_mod // tm
    return pl.pallas_call(
        functools.partial(_outproj_kernel, final_norm=final_norm),
        out_shape=jax.ShapeDtypeStruct((r, d), F32),
        grid=(r // tm,),
        in_specs=[pl.BlockSpec((tm, half), lambda i: (i, 0)),
                  pl.BlockSpec((tm, half), lambda i: (i, 0)),
                  pl.BlockSpec((2 * half, d), lambda i: (0, 0)),
                  pl.BlockSpec((tm, d), lambda i: (i, 0)),
                  pl.BlockSpec((1, rb, d), lambda i: (i // tiles_per_mod, 0, 0)),
                  pl.BlockSpec((1, d), lambda i: (0, 0))],
        out_specs=pl.BlockSpec((tm, d), lambda i: (i, 0)),
        compiler_params=_params("arbitrary"),
        name="outproj",
    )(m_att, m_hg, w_bf, x, rg, final_g.reshape(1, d))


def _tile_plan(seq, n_pages):
    return {"rows": math.gcd(seq, 1024), "out_rows": math.gcd(seq, 512),
            "pages": math.gcd(n_pages, SAMPLE_PAGES_PER_STEP)}


def kernel(x_prompt, x_sample, cache_k, cache_v, state_hgrn, page_table, c_prompt, c_sample,
           norm_g, w_ada, b_ada, w_in, lambda_q1, lambda_k1, lambda_q2, lambda_k2, subln_g,
           hg_lower_bounds, hg_norm_g, w_out, final_g):
    batch, seq, d = x_prompt.shape
    dec_batch, n_new, _ = x_sample.shape
    depth, n_pool, page, n_heads, _, qk_dim = cache_k.shape
    hg_heads = state_hgrn.shape[2]
    width = d // 2
    assert qk_dim == QK_DIM and n_heads * HEAD_DIM == width and hg_heads * HEAD_DIM == width
    assert w_in.shape[2] == N_SECTIONS * width
    tiles = _tile_plan(seq, page_table.shape[1])

    slopes = jnp.asarray([2.0 ** (-8.0 * (h + 1) / n_heads) for h in range(n_heads)], F32)
    lb_all = jnp.cumsum(jax.nn.softmax(hg_lower_bounds.astype(F32), axis=0), axis=0)
    n_mod = batch + dec_batch
    n_mod_pad = -(-n_mod // 8) * 8
    c_all = jnp.concatenate([c_prompt, c_sample, jnp.zeros((n_mod_pad - n_mod, d), F32)], axis=0)

    xp = x_prompt.reshape(batch * seq, d)
    xs = x_sample.reshape(dec_batch * n_new, d)
    rs = dec_batch * n_new
    outs = [[] for _ in range(6)]
    for l in range(depth):
        last = l == depth - 1
        lam_init = 0.8 - 0.6 * math.exp(-0.3 * l)
        lam = (jnp.exp(jnp.sum(lambda_q1[l].astype(F32) * lambda_k1[l].astype(F32)))
               - jnp.exp(jnp.sum(lambda_q2[l].astype(F32) * lambda_k2[l].astype(F32))) + lam_init)
        par = jnp.concatenate([lam.reshape(1), slopes, jnp.zeros((7,), F32)])
        w_out_bf = w_out[l].astype(BF16)

        mod = _adaln(c_all, w_ada[l], b_ada[l])
        shift, scale, rgate = mod[:, :d], mod[:, d:2 * d], mod[:, 2 * d:]

        def rows_p(t):
            return t[:batch].reshape(batch, 1, d)

        def rows_s(t):
            return jnp.repeat(t[batch:n_mod], n_new, axis=0).reshape(1, rs, d)

        t = tiles
        hs = _modnorm(xs, norm_g[l], rows_s(scale), rows_s(shift), tm=rs, rows_per_mod=rs)
        zs, w_in_bf = _inproj(hs, w_in[l])
        hp = _modnorm(xp, norm_g[l], rows_p(scale), rows_p(shift), tm=t["rows"], rows_per_mod=seq)
        qkv, gh, k_t, v_rows = _inproj_prompt(hp, w_in_bf, batch, seq, n_heads, tm=t["rows"])
        cache_kt = jnp.transpose(cache_k[l], (0, 2, 3, 4, 1)).reshape(n_pool, width, page)
        cache_vr = cache_v[l].reshape(n_pool, page * n_heads, HEAD_DIM)
        att_s = _attn_sample(zs.reshape(N_SECTIONS, dec_batch, n_new, width), cache_kt, cache_vr,
                             page_table, par, subln_g[l], n_heads, n_new, 1.0 - lam_init, npg=t["pages"])
        att_p = _attn_prompt(qkv, gh, par, subln_g[l], batch, seq, n_heads, 1.0 - lam_init,
                             tq=t["rows"], tk=t["rows"] // 2)
        hg_p, s_p = _hgrn(gh, 1, lb_all[l], hg_norm_g[l], None, batch, seq, hg_heads,
                          heads_per_step=2 if hg_heads % 2 == 0 else 1,
                          tb=t["rows"], chunk=math.gcd(seq, HG_CHUNK), out_dtype=BF16)
        xp = _outproj(att_p, hg_p[0], w_out_bf, xp, rows_p(rgate), final_g,
                      tm=t["out_rows"], rows_per_mod=seq, final_norm=last)
        k_p = jnp.transpose(k_t.reshape(batch, n_heads, 2, QK_DIM, seq), (0, 4, 1, 2, 3))
        outs[0].append(k_p)
        outs[1].append(v_rows.reshape(batch, seq, n_heads, HEAD_DIM))
        outs[2].append(s_p)

        hg_s, s_s = _hgrn(zs, 4, lb_all[l], hg_norm_g[l], state_hgrn[l], dec_batch, n_new, hg_heads,
                          heads_per_step=hg_heads, tb=n_new, chunk=n_new, out_dtype=F32,
                          seqs_per_step=math.gcd(dec_batch, 8))
        xs = _outproj(att_s.reshape(rs, width), hg_s[0], w_out_bf, xs, rows_s(rgate), final_g,
                      tm=rs, rows_per_mod=rs, final_norm=last)
        outs[3].append(zs[1].reshape(dec_batch, n_new, n_heads, 2, QK_DIM))
        outs[4].append(zs[2].reshape(dec_batch, n_new, n_heads, HEAD_DIM))
        outs[5].append(s_s)

    y_prompt = xp.reshape(batch, seq, d)
    y_sample = xs.reshape(dec_batch, n_new, d)
    return (y_prompt, y_sample) + tuple(jnp.stack(o) for o in outs)
```
